```python
import jax, jax.numpy as jnp
from jax import lax
import numpy as np

D_MODEL = 1024
BATCH = 8
SEQ = 4096
DEPTH = 2
DEC_BATCH = 32
DEC_SEQ = 4
PAST_LEN = 16384
PAGE_SIZE = 128

HEAD_DIM = 64
MIX_WIDTH = D_MODEL
A_HEADS = MIX_WIDTH // (4 * HEAD_DIM)
B_HEADS = MIX_WIDTH // (2 * HEAD_DIM)
C_HEADS = MIX_WIDTH // (4 * HEAD_DIM)
A_WIDTH = A_HEADS * HEAD_DIM
B_WIDTH = B_HEADS * HEAD_DIM
C_WIDTH = C_HEADS * HEAD_DIM
CHUNK = 128
Q_BLOCK = 128
RET_CHUNK = 128
D_FF = ((8 * D_MODEL // 3 + 127) // 128) * 128
ROPE_BASE = 10000.0
EPS = 1e-6
N_SUBLAYERS = 3
IN_SIZES = (A_WIDTH, A_WIDTH, B_WIDTH, B_WIDTH, B_WIDTH, B_HEADS, C_WIDTH, C_WIDTH, C_WIDTH, C_WIDTH)
IN_COLS = sum(IN_SIZES)

kernel_name = "hymba_gmlp_fox_retnet_macaron_step"

F32 = jnp.float32


def rms_norm(x, g=None):
    xf = x.astype(F32)
    y = xf * lax.rsqrt(jnp.mean(xf * xf, axis=-1, keepdims=True) + EPS)
    if g is not None:
        y = y * g.astype(F32)
    return y.astype(x.dtype)


def adaln(c, w, b):
    m = jax.nn.silu(c) @ w + b
    return m.reshape(c.shape[0], N_SUBLAYERS, 3, D_MODEL)


def modulated_norm(x, g, shift, scale):
    return rms_norm(x, g) * (1.0 + scale[:, None, :]) + shift[:, None, :]


def half_ffn(x, m, g, wg, wu, wd):
    h = modulated_norm(x, g, m[:, 0], m[:, 1])
    y = (jax.nn.silu(h @ wg) * (h @ wu)) @ wd
    return x + 0.5 * m[:, 2][:, None, :] * y


def rope(x, pos):
    half = HEAD_DIM // 2
    inv = ROPE_BASE ** (-jnp.arange(half, dtype=F32) / half)
    ang = pos[:, None] * inv[None, :]
    cos = jnp.cos(ang)[:, None, :]
    sin = jnp.sin(ang)[:, None, :]
    xf = x.astype(F32)
    x1, x2 = xf[..., :half], xf[..., half:]
    return jnp.concatenate([x1 * cos - x2 * sin, x1 * sin + x2 * cos], axis=-1).astype(x.dtype)


def project(h, w_in_l, g_a_v_l, g_q, g_k, b_f, pos):
    bsz, t = h.shape[0], h.shape[1]
    z = h @ w_in_l
    cuts = [int(v) for v in np.cumsum(IN_SIZES)[:-1]]
    a_u, a_v, b_q, b_k, b_v, b_fl, c_q, c_k, c_v, c_g = jnp.split(z, cuts, axis=-1)
    heads = lambda u, n: u.reshape(bsz, t, n, HEAD_DIM)
    a_u = jax.nn.gelu(a_u)
    a_v = rms_norm(jax.nn.gelu(a_v), g_a_v_l)
    q_b = rms_norm(heads(b_q, B_HEADS), g_q)
    k_b = rms_norm(heads(b_k, B_HEADS), g_k)
    v_b = heads(b_v, B_HEADS)
    logf = jax.nn.log_sigmoid((b_fl + b_f).astype(F32))
    q_c = rope(heads(c_q, C_HEADS), pos)
    k_c = rope(heads(c_k, C_HEADS), pos) * (HEAD_DIM ** -0.5)
    v_c = heads(c_v, C_HEADS)
    return a_u, a_v, q_b, k_b, v_b, logf, q_c, k_c, v_c, c_g


def chunk_mlp(u, v, w_s, b_s):
    bsz, t = v.shape[0], v.shape[1]
    L = min(t, CHUNK)
    n = t // L
    w = w_s[:, :L, :L] * jnp.tril(jnp.ones((L, L), w_s.dtype))
    vh = v.reshape(bsz, n, L, A_HEADS, HEAD_DIM)
    s = jnp.einsum('hij,bnjhd->bnihd', w, vh) + b_s[:, :L].T[:, :, None]
    return u.reshape(bsz, t, A_HEADS, HEAD_DIM) * s.reshape(bsz, t, A_HEADS, HEAD_DIM)


def fox_prompt(q, k, v, logf):
    bsz, t, nh, hd = q.shape
    nb = t // Q_BLOCK
    scale = hd ** -0.5
    cum = jnp.cumsum(logf, axis=1)
    cum_k = cum.transpose(0, 2, 1)
    kpos = jnp.arange(t)
    qb = q.reshape(bsz, nb, Q_BLOCK, nh, hd).transpose(1, 0, 2, 3, 4)
    cqb = cum.reshape(bsz, nb, Q_BLOCK, nh).transpose(1, 0, 3, 2)

    def block(args):
        qi, cqi, bi = args
        s = jnp.einsum('bqhd,bkhd->bhqk', qi, k, preferred_element_type=F32) * scale
        s = s + cqi[..., None] - cum_k[:, :, None, :]
        qpos = bi * Q_BLOCK + jnp.arange(Q_BLOCK)
        s = jnp.where(kpos[None, :] <= qpos[:, None], s, -jnp.inf)
        p = jax.nn.softmax(s, axis=-1)
        return jnp.einsum('bhqk,bkhd->bqhd', p, v.astype(F32))

    out = lax.map(block, (qb, cqb, jnp.arange(nb)))
    return out.transpose(1, 0, 2, 3, 4).reshape(bsz, t, nh, hd).astype(q.dtype)


def fox_sample(q, k_new, v_new, logf_new, k_pool, v_pool, logf_pool, page_table):
    db, t, nh, hd = q.shape
    past = page_table.shape[1] * k_pool.shape[1]
    scale = hd ** -0.5
    k_past = k_pool[page_table].reshape(db, past, nh, hd)
    v_past = v_pool[page_table].reshape(db, past, nh, hd)
    lf_past = logf_pool[page_table].reshape(db, past, nh).astype(F32)
    cum_past = jnp.cumsum(lf_past, axis=1)
    cum_new = cum_past[:, -1:] + jnp.cumsum(logf_new, axis=1)
    cq = cum_new.transpose(0, 2, 1)[..., None]
    s_past = jnp.einsum('bqhd,bkhd->bhqk', q, k_past, preferred_element_type=F32) * scale
    s_past = s_past + cq - cum_past.transpose(0, 2, 1)[:, :, None, :]
    s_new = jnp.einsum('bqhd,bkhd->bhqk', q, k_new, preferred_element_type=F32) * scale
    s_new = s_new + cq - cum_new.transpose(0, 2, 1)[:, :, None, :]
    causal = jnp.tril(jnp.ones((t, t), dtype=bool))
    s_new = jnp.where(causal, s_new, -jnp.inf)
    p = jax.nn.softmax(jnp.concatenate([s_past, s_new], axis=-1), axis=-1)
    out = jnp.einsum('bhqk,bkhd->bqhd', p[..., :past], v_past.astype(F32)) \
        + jnp.einsum('bhqk,bkhd->bqhd', p[..., past:], v_new.astype(F32))
    return out.astype(q.dtype)


def retention(q, k, v, s0, log_gamma):
    bsz, t, nh, hd = q.shape
    L = min(t, RET_CHUNK)
    n = t // L
    to_chunks = lambda u: u.astype(F32).reshape(bsz, n, L, nh, hd).transpose(1, 0, 2, 3, 4)
    idx = jnp.arange(L, dtype=F32)
    rel = idx[:, None] - idx[None, :]
    d_intra = jnp.where(rel[None] >= 0, jnp.exp(jnp.maximum(rel, 0.0)[None] * log_gamma[:, None, None]), 0.0)
    d_q = jnp.exp((idx + 1.0)[:, None] * log_gamma[None, :])[None, :, :, None]
    d_k = jnp.exp((L - 1.0 - idx)[:, None] * log_gamma[None, :])[None, :, :, None]
    d_chunk = jnp.exp(L * log_gamma)[None, :, None, None]

    def step(S, inp):
        qi, ki, vi = inp
        att = jnp.einsum('bihd,bjhd->bhij', qi, ki) * d_intra
        intra = jnp.einsum('bhij,bjhe->bihe', att, vi)
        inter = jnp.einsum('bihd,bhde->bihe', qi, S) * d_q
        S = S * d_chunk + jnp.einsum('bjhd,bjhe->bhde', ki * d_k, vi)
        return S, intra + inter

    S, out = lax.scan(step, s0.astype(F32), (to_chunks(q), to_chunks(k), to_chunks(v)))
    out = out.transpose(1, 0, 2, 3, 4).reshape(bsz, t, nh, hd).astype(q.dtype)
    return out, S


def merge(o_a, o_b, o_c, c_g, g_mix_l, w_out_l):
    bsz, t = o_a.shape[0], o_a.shape[1]
    o = jnp.concatenate([o_a, o_b, o_c], axis=2)
    o = rms_norm(o).reshape(bsz, t, MIX_WIDTH) * g_mix_l
    ab = A_WIDTH + B_WIDTH
    o = jnp.concatenate([o[..., :ab], o[..., ab:] * jax.nn.silu(c_g)], axis=-1)
    return o @ w_out_l


def setup_inputs(seed: int = 0) -> dict:
    key = jax.random.key(seed)
    ks = jax.random.split(key, 32)
    n_pages = PAST_LEN // PAGE_SIZE
    n_pool = (DEC_BATCH * n_pages * 5) // 4
    nrm = lambda k, shape, s: jax.random.normal(k, shape, F32) * s
    page_table = jax.random.permutation(ks[6], n_pool)[:DEC_BATCH * n_pages].reshape(DEC_BATCH, n_pages).astype(jnp.int32)
    return {
        "x_prompt": nrm(ks[0], (BATCH, SEQ, D_MODEL), 1.0),
        "x_sample": nrm(ks[1], (DEC_BATCH, DEC_SEQ, D_MODEL), 1.0),
        "cache_k": nrm(ks[2], (DEPTH, n_pool, PAGE_SIZE, B_HEADS, HEAD_DIM), 1.0),
        "cache_v": nrm(ks[3], (DEPTH, n_pool, PAGE_SIZE, B_HEADS, HEAD_DIM), 1.0),
        "cache_logf": jax.nn.log_sigmoid(3.0 + nrm(ks[4], (DEPTH, n_pool, PAGE_SIZE, B_HEADS), 0.5)),
        "state_ret": nrm(ks[5], (DEPTH, DEC_BATCH, C_HEADS, HEAD_DIM, HEAD_DIM), 0.5),
        "page_table": page_table,
        "c_prompt": nrm(ks[7], (BATCH, D_MODEL), 1.0),
        "c_sample": nrm(ks[8], (DEC_BATCH, D_MODEL), 1.0),
        "g_norm": 1.0 + nrm(ks[9], (DEPTH, N_SUBLAYERS, D_MODEL), 0.05),
        "w_ada": nrm(ks[10], (DEPTH, D_MODEL, N_SUBLAYERS * 3 * D_MODEL), D_MODEL ** -0.5),
        "b_ada": nrm(ks[11], (DEPTH, N_SUBLAYERS * 3 * D_MODEL), 0.02),
        "w_ffn_gate": nrm(ks[12], (DEPTH, 2, D_MODEL, D_FF), D_MODEL ** -0.5),
        "w_ffn_up": nrm(ks[13], (DEPTH, 2, D_MODEL, D_FF), D_MODEL ** -0.5),
        "w_ffn_down": nrm(ks[14], (DEPTH, 2, D_FF, D_MODEL), D_FF ** -0.5),
        "w_in": nrm(ks[15], (DEPTH, D_MODEL, IN_COLS), D_MODEL ** -0.5),
        "g_a_v": 1.0 + nrm(ks[16], (DEPTH, A_WIDTH), 0.05),
        "w_spatial": nrm(ks[17], (DEPTH, A_HEADS, CHUNK, CHUNK), CHUNK ** -0.5),
        "b_spatial": 1.0 + nrm(ks[18], (DEPTH, A_HEADS, CHUNK), 0.1),
        "g_qnorm": 1.0 + nrm(ks[19], (DEPTH, HEAD_DIM), 0.05),
        "g_knorm": 1.0 + nrm(ks[20], (DEPTH, HEAD_DIM), 0.05),
        "b_forget": 3.0 + nrm(ks[21], (DEPTH, B_HEADS), 0.5),
        "g_mix": 1.0 + nrm(ks[22], (DEPTH, MIX_WIDTH), 0.05),
        "w_out": nrm(ks[23], (DEPTH, MIX_WIDTH, D_MODEL), MIX_WIDTH ** -0.5),
    }


def reference(x_prompt, x_sample, cache_k, cache_v, cache_logf, state_ret, page_table, c_prompt, c_sample,
              g_norm, w_ada, b_ada, w_ffn_gate, w_ffn_up, w_ffn_down, w_in, g_a_v, w_spatial, b_spatial,
              g_qnorm, g_knorm, b_forget, g_mix, w_out):
    past_len = page_table.shape[1] * cache_k.shape[2]
    pos_p = jnp.arange(x_prompt.shape[1], dtype=F32)
    pos_s = past_len + jnp.arange(x_sample.shape[1], dtype=F32)
    log_gamma = jnp.log1p(-jnp.exp2(-5.0 - jnp.arange(C_HEADS, dtype=F32)))
    x_p, x_s = x_prompt, x_sample
    kp, vp, lp, rp = [], [], [], []
    ksm, vsm, lsm, rsm, avs = [], [], [], [], []
    for l in range(DEPTH):
        mp = adaln(c_prompt, w_ada[l], b_ada[l])
        ms = adaln(c_sample, w_ada[l], b_ada[l])
        x_p = half_ffn(x_p, mp[:, 0], g_norm[l, 0], w_ffn_gate[l, 0], w_ffn_up[l, 0], w_ffn_down[l, 0])
        x_s = half_ffn(x_s, ms[:, 0], g_norm[l, 0], w_ffn_gate[l, 0], w_ffn_up[l, 0], w_ffn_down[l, 0])
        h_p = modulated_norm(x_p, g_norm[l, 1], mp[:, 1, 0], mp[:, 1, 1])
        a_u, a_v, q_b, k_b, v_b, logf, q_c, k_c, v_c, c_g = project(h_p, w_in[l], g_a_v[l], g_qnorm[l], g_knorm[l], b_forget[l], pos_p)
        o_a = chunk_mlp(a_u, a_v, w_spatial[l], b_spatial[l])
        o_b = fox_prompt(q_b, k_b, v_b, logf)
        o_c, s_p = retention(q_c, k_c, v_c, jnp.zeros((x_p.shape[0], C_HEADS, HEAD_DIM, HEAD_DIM), F32), log_gamma)
        x_p = x_p + mp[:, 1, 2][:, None, :] * merge(o_a, o_b, o_c, c_g, g_mix[l], w_out[l])
        kp.append(k_b); vp.append(v_b); lp.append(logf.astype(cache_logf.dtype)); rp.append(s_p.astype(state_ret.dtype))
        h_s = modulated_norm(x_s, g_norm[l, 1], ms[:, 1, 0], ms[:, 1, 1])
        a_u, a_v, q_b, k_b, v_b, logf, q_c, k_c, v_c, c_g = project(h_s, w_in[l], g_a_v[l], g_qnorm[l], g_knorm[l], b_forget[l], pos_s)
        o_a = chunk_mlp(a_u, a_v, w_spatial[l], b_spatial[l])
        o_b = fox_sample(q_b, k_b, v_b, logf, cache_k[l], cache_v[l], cache_logf[l], page_table)
        o_c, s_s = retention(q_c, k_c, v_c, state_ret[l], log_gamma)
        x_s = x_s + ms[:, 1, 2][:, None, :] * merge(o_a, o_b, o_c, c_g, g_mix[l], w_out[l])
        ksm.append(k_b); vsm.append(v_b); lsm.append(logf.astype(cache_logf.dtype)); rsm.append(s_s.astype(state_ret.dtype)); avs.append(a_v)
        x_p = half_ffn(x_p, mp[:, 2], g_norm[l, 2], w_ffn_gate[l, 1], w_ffn_up[l, 1], w_ffn_down[l, 1])
        x_s = half_ffn(x_s, ms[:, 2], g_norm[l, 2], w_ffn_gate[l, 1], w_ffn_up[l, 1], w_ffn_down[l, 1])
    return (x_p, x_s, jnp.stack(kp), jnp.stack(vp), jnp.stack(lp), jnp.stack(rp),
            jnp.stack(ksm), jnp.stack(vsm), jnp.stack(lsm), jnp.stack(rsm), jnp.stack(avs))
```

```python
import functools

import jax
import jax.numpy as jnp
from jax import lax
from jax.experimental import pallas as pl
from jax.experimental.pallas import tpu as pltpu

F32 = jnp.float32
BF16 = jnp.bfloat16

HEAD_DIM = 64
A_HEADS, B_HEADS, C_HEADS = 4, 8, 4
A_WIDTH, B_WIDTH, C_WIDTH = A_HEADS * HEAD_DIM, B_HEADS * HEAD_DIM, C_HEADS * HEAD_DIM
CHUNK = 128
ROPE_BASE = 10000.0
EPS = 1e-6
N_SUBLAYERS = 3
LANES = 128
MXU_DIM = 256
NEG_BIG = -1e30
VMEM_LIMIT = 56 * 1024 * 1024


def _dot(a, b):
    return jnp.dot(a, b, preferred_element_type=F32)


def _dot_nt(a, b):
    return lax.dot_general(a, b, (((1,), (1,)), ((), ())), preferred_element_type=F32)


def _silu(x):
    return x / (1.0 + jnp.exp(-x))


def _gelu_tanh(x):
    c = 0.7978845608028654
    return 0.5 * x * (1.0 + jnp.tanh(c * (x + 0.044715 * (x * x * x))))


def _log_sigmoid(x):
    return jnp.minimum(x, 0.0) - jnp.log1p(jnp.exp(-jnp.abs(x)))


def _modnorm(x, g, shift, scale):
    ms = jnp.mean(x * x, axis=-1, keepdims=True)
    return (x * lax.rsqrt(ms + EPS) * g) * (1.0 + scale) + shift


def _split_bf16(x, parts):
    out = []
    r = x
    for i in range(parts):
        p = r.astype(BF16)
        out.append(p)
        if i + 1 < parts:
            r = r - p.astype(F32)
    return out


def _dot_split_lhs(a, b_bf16, parts):
    acc = None
    for p in _split_bf16(a, parts):
        t = _dot(p, b_bf16)
        acc = t if acc is None else acc + t
    return acc


def _dot_split_rhs(a_bf16, b, parts):
    acc = None
    for p in _split_bf16(b, parts):
        t = _dot(a_bf16, p)
        acc = t if acc is None else acc + t
    return acc


def _head_mean_sq(x, seg):
    w = x.shape[-1]
    outs = []
    for c in range(w // MXU_DIM):
        xs = x[:, c * MXU_DIM:(c + 1) * MXU_DIM]
        outs.append(_dot_split_lhs(xs * xs, seg, 2))
    ss = outs[0] if len(outs) == 1 else jnp.concatenate(outs, axis=-1)
    return ss * (1.0 / HEAD_DIM)


def _head_norm(x, seg):
    return x * lax.rsqrt(_head_mean_sq(x, seg) + EPS)


def _rope(x, cos, sin_signed):
    outs = []
    for c in range(x.shape[-1] // LANES):
        xs = x[:, c * LANES:(c + 1) * LANES]
        fwd = pltpu.roll(xs, HEAD_DIM // 2, axis=1)
        bwd = pltpu.roll(xs, LANES - HEAD_DIM // 2, axis=1)
        lane = lax.broadcasted_iota(jnp.int32, xs.shape, 1)
        outs.append(jnp.where((lane % HEAD_DIM) < HEAD_DIM // 2, bwd, fwd))
    partner = jnp.concatenate(outs, axis=-1)
    return x * cos + partner * sin_signed


def _tile_rows_masked(x, nrep, rows_per, cols_per):
    xt = jnp.concatenate([x] * nrep, axis=0)
    r = lax.broadcasted_iota(jnp.int32, xt.shape, 0) // rows_per
    c = lax.broadcasted_iota(jnp.int32, xt.shape, 1) // cols_per
    return jnp.where(r == c, xt, 0.0)


def _tile_cols_masked(x, nrep, rows_per, cols_per):
    xt = jnp.concatenate([x] * nrep, axis=1)
    r = lax.broadcasted_iota(jnp.int32, xt.shape, 0) // rows_per
    c = lax.broadcasted_iota(jnp.int32, xt.shape, 1) // cols_per
    return jnp.where(r == c, xt, 0.0)


def _ada_kernel(c_ref, w_ref, b_ref, o_ref):
    a = _silu(c_ref[...]).astype(BF16)
    o_ref[...] = _dot(a, w_ref[...].astype(BF16)) + b_ref[...]


def _ada_call(c_all, w_ada, b_ada):
    depth, d, ncol = w_ada.shape
    rows = c_all.shape[0]
    tn = 1024
    return pl.pallas_call(
        _ada_kernel,
        grid=(depth, ncol // tn),
        in_specs=[
            pl.BlockSpec((rows, d), lambda l, n: (0, 0)),
            pl.BlockSpec((None, d, tn), lambda l, n: (l, 0, n)),
            pl.BlockSpec((None, 1, tn), lambda l, n: (l, 0, n)),
        ],
        out_specs=pl.BlockSpec((None, rows, tn), lambda l, n: (l, 0, n)),
        out_shape=jax.ShapeDtypeStruct((depth, rows, ncol), F32),
        compiler_params=pltpu.CompilerParams(
            dimension_semantics=("arbitrary", "arbitrary"), vmem_limit_bytes=VMEM_LIMIT),
        name="adaln",
    )(c_all, w_ada, b_ada.reshape(depth, 1, ncol))


def _mod_specs(per_row, tm, d, tiles_per_seq, sub, which):
    col = sub * 3 + which
    if per_row:
        return pl.BlockSpec((tm, d), lambda i: (i, col))
    return pl.BlockSpec((None, 1, d), lambda i: (i // tiles_per_seq, 0, col))


def _ffn_kernel(x_ref, sh_ref, sc_ref, gt_ref, g_ref, wg_ref, wu_ref, wd_ref, o_ref, *, nf):
    x = x_ref[...]
    h = _modnorm(x, g_ref[...], sh_ref[...], sc_ref[...]).astype(BF16)
    acc = None
    for f in range(nf):
        gate = _dot(h, wg_ref[f])
        up = _dot(h, wu_ref[f])
        a = (_silu(gate) * up).astype(BF16)
        t = _dot(a, wd_ref[f])
        acc = t if acc is None else acc + t
    o_ref[...] = x + (0.5 * gt_ref[...]) * acc


def _ffn_call(x2, mod, sub, g, wg3, wu3, wd3, *, per_row, tm, seq_len):
    n, d = x2.shape
    nf, _, tf = wg3.shape
    tiles_per_seq = seq_len // tm if not per_row else 1
    const3 = lambda i: (0, 0, 0)
    return pl.pallas_call(
        functools.partial(_ffn_kernel, nf=nf),
        grid=(n // tm,),
        in_specs=[
            pl.BlockSpec((tm, d), lambda i: (i, 0)),
            _mod_specs(per_row, tm, d, tiles_per_seq, sub, 0),
            _mod_specs(per_row, tm, d, tiles_per_seq, sub, 1),
            _mod_specs(per_row, tm, d, tiles_per_seq, sub, 2),
            pl.BlockSpec((1, d), lambda i: (0, 0)),
            pl.BlockSpec((nf, d, tf), const3, pipeline_mode=pl.Buffered(1)),
            pl.BlockSpec((nf, d, tf), const3, pipeline_mode=pl.Buffered(1)),
            pl.BlockSpec((nf, tf, d), const3, pipeline_mode=pl.Buffered(1)),
        ],
        out_specs=pl.BlockSpec((tm, d), lambda i: (i, 0)),
        out_shape=jax.ShapeDtypeStruct((n, d), F32),
        compiler_params=pltpu.CompilerParams(
            dimension_semantics=("arbitrary",), vmem_limit_bytes=VMEM_LIMIT),
        name="half_ffn",
    )(x2, mod, mod, mod, g, wg3, wu3, wd3)


def _proj_prompt_kernel(
        x_ref, sh_ref, sc_ref, g_ref, wa_ref, wb_ref, wf_ref, wc_ref,
        gav_ref, gq_ref, gk_ref, bf_ref, gma_ref, gmc_ref, cos_ref, sin_ref,
        ws_ref, bsp_ref, dintra_ref, dq_ref, dk_ref, dch_ref, seg_ref, tril_ref, triu_ref,
        k32_ref, v32_ref, logf_ref, q16_ref, k16_ref, v16_ref, cumc_ref, cumr_ref,
        oa_ref, oc_ref, sret_ref,
        s_scr, cc_scr, cr_scr, *, tm, ta):
    j = pl.program_id(1)

    @pl.when(j == 0)
    def _():
        s_scr[...] = jnp.zeros_like(s_scr)
        cc_scr[...] = jnp.zeros_like(cc_scr)
        cr_scr[...] = jnp.zeros_like(cr_scr)

    x = x_ref[...]
    hb = _modnorm(x, g_ref[...], sh_ref[...], sc_ref[...]).astype(BF16)
    seg = seg_ref[...]

    za = _dot(hb, wa_ref[...])
    a_u = _gelu_tanh(za[:, :A_WIDTH])
    a_vr = _gelu_tanh(za[:, A_WIDTH:])
    a_v = a_vr * lax.rsqrt(jnp.mean(a_vr * a_vr, axis=-1, keepdims=True) + EPS) * gav_ref[...]

    zb = _dot(hb, wb_ref[...])
    q_b = _head_norm(zb[:, :B_WIDTH], seg) * gq_ref[...]
    k_b = _head_norm(zb[:, B_WIDTH:2 * B_WIDTH], seg) * gk_ref[...]
    v_b = zb[:, 2 * B_WIDTH:]
    k32_ref[...] = k_b
    v32_ref[...] = v_b
    q16_ref[...] = q_b.astype(BF16)
    k16_ref[...] = k_b.astype(BF16)
    v16_ref[...] = v_b.astype(BF16)

    logf = _log_sigmoid(_dot(hb, wf_ref[...]) + bf_ref[...])
    logf_ref[...] = logf[:, :B_HEADS]
    cum_c = _dot_split_rhs(tril_ref[...], logf, 3) + cc_scr[...]
    cc_scr[...] = cum_c[tm - 1:tm, :]
    for h in range(B_HEADS):
        cumc_ref[h] = cum_c[:, h:h + 1]
    logf_t = jnp.transpose(logf)[:B_HEADS, :]
    cum_r = _dot_split_lhs(logf_t, triu_ref[...], 3) + cr_scr[:, 0:1]
    cr_scr[...] = jnp.broadcast_to(cum_r[:, tm - 1:tm], cr_scr.shape)
    for h in range(B_HEADS):
        for c in range(tm // ta):
            cumr_ref[h, c] = cum_r[h:h + 1, c * ta:(c + 1) * ta]

    zc = _dot(hb, wc_ref[...])
    cos = cos_ref[...]
    sin = sin_ref[...]
    q_c = _rope(zc[:, :C_WIDTH], cos, sin)
    k_c = _rope(zc[:, C_WIDTH:2 * C_WIDTH], cos, sin) * (HEAD_DIM ** -0.5)
    v_c = zc[:, 2 * C_WIDTH:3 * C_WIDTH]
    c_g = zc[:, 3 * C_WIDTH:]

    row = lax.broadcasted_iota(jnp.int32, (CHUNK, CHUNK), 0)
    col = lax.broadcasted_iota(jnp.int32, (CHUNK, CHUNK), 1)
    w_cat = jnp.concatenate(
        [jnp.where(col <= row, ws_ref[h], 0.0) for h in range(A_HEADS)], axis=1).astype(BF16)
    dintra = dintra_ref[...]
    dq = dq_ref[...]
    dk = dk_ref[...]
    dch = dch_ref[...]
    bsp = bsp_ref[...]
    sr = lax.broadcasted_iota(jnp.int32, (C_WIDTH, C_WIDTH), 0) // HEAD_DIM
    scol = lax.broadcasted_iota(jnp.int32, (C_WIDTH, C_WIDTH), 1) // HEAD_DIM
    s_mask = sr == scol
    s_blk = s_scr[...]
    oa_parts = []
    oc_parts = []
    for c in range(tm // CHUNK):
        sl = slice(c * CHUNK, (c + 1) * CHUNK)
        av_blk = _tile_rows_masked(a_v[sl], A_HEADS, CHUNK, HEAD_DIM).astype(BF16)
        oa_parts.append(a_u[sl] * (_dot(w_cat, av_blk) + bsp))

        qc = q_c[sl].astype(BF16)
        k_blk = _tile_rows_masked(k_c[sl], C_HEADS, CHUNK, HEAD_DIM).astype(BF16)
        v_blk = _tile_rows_masked(v_c[sl], C_HEADS, CHUNK, HEAD_DIM).astype(BF16)
        att = _dot_nt(qc, k_blk) * dintra
        intra = _dot(att.astype(BF16), v_blk)
        inter = _dot(qc, s_blk.astype(BF16)) * dq
        oc_parts.append(intra + inter)
        kd_t = jnp.transpose(k_c[sl] * dk).astype(BF16)
        upd = _dot(kd_t, v_c[sl].astype(BF16))
        s_blk = s_blk * dch + jnp.where(s_mask, upd, 0.0)
    s_scr[...] = s_blk
    sret_ref[...] = s_blk

    o_a = oa_parts[0] if len(oa_parts) == 1 else jnp.concatenate(oa_parts, axis=0)
    o_c = oc_parts[0] if len(oc_parts) == 1 else jnp.concatenate(oc_parts, axis=0)
    oa_ref[...] = (_head_norm(o_a, seg) * gma_ref[...]).astype(BF16)
    oc_ref[...] = (_head_norm(o_c, seg) * gmc_ref[...] * _silu(c_g)).astype(BF16)


def _proj_prompt_call(x3, mod, g, w, cst, *, tm, ta):
    b, t, d = x3.shape
    nt = t // tm
    full2 = lambda bb, j: (0, 0)
    full3 = lambda bb, j: (0, 0, 0)
    row_spec = lambda width: pl.BlockSpec((None, tm, width), lambda bb, j: (bb, j, 0))
    mod_spec = lambda col: pl.BlockSpec((None, 1, d), lambda bb, j: (bb, 0, col))
    vec = lambda width: pl.BlockSpec((1, width), full2)
    in_specs = [
        row_spec(d), mod_spec(3), mod_spec(4), vec(d),
        pl.BlockSpec((d, 2 * A_WIDTH), full2), pl.BlockSpec((d, 3 * B_WIDTH), full2),
        pl.BlockSpec((d, LANES), full2), pl.BlockSpec((d, 4 * C_WIDTH), full2),
        vec(A_WIDTH), vec(B_WIDTH), vec(B_WIDTH), vec(LANES), vec(A_WIDTH), vec(C_WIDTH),
        pl.BlockSpec((tm, C_WIDTH), lambda bb, j: (j, 0)),
        pl.BlockSpec((tm, C_WIDTH), lambda bb, j: (j, 0)),
        pl.BlockSpec((A_HEADS, CHUNK, CHUNK), full3),
        pl.BlockSpec((CHUNK, A_WIDTH), full2),
        pl.BlockSpec((CHUNK, C_HEADS * CHUNK), full2),
        pl.BlockSpec((CHUNK, C_WIDTH), full2), pl.BlockSpec((CHUNK, C_WIDTH), full2),
        vec(C_WIDTH),
        pl.BlockSpec((MXU_DIM, MXU_DIM), full2),
        pl.BlockSpec((tm, tm), full2), pl.BlockSpec((tm, tm), full2),
    ]
    out_shape = [
        jax.ShapeDtypeStruct((b, t, B_WIDTH), F32), jax.ShapeDtypeStruct((b, t, B_WIDTH), F32),
        jax.ShapeDtypeStruct((b, t, B_HEADS), F32),
        jax.ShapeDtypeStruct((b, t, B_WIDTH), BF16), jax.ShapeDtypeStruct((b, t, B_WIDTH), BF16),
        jax.ShapeDtypeStruct((b, t, B_WIDTH), BF16),
        jax.ShapeDtypeStruct((b, B_HEADS, t, 1), F32),
        jax.ShapeDtypeStruct((b, B_HEADS, t // ta, 1, ta), F32),
        jax.ShapeDtypeStruct((b, t, A_WIDTH), BF16), jax.ShapeDtypeStruct((b, t, C_WIDTH), BF16),
        jax.ShapeDtypeStruct((b, C_WIDTH, C_WIDTH), F32),
    ]
    out_specs = [
        row_spec(B_WIDTH), row_spec(B_WIDTH), row_spec(B_HEADS),
        row_spec(B_WIDTH), row_spec(B_WIDTH), row_spec(B_WIDTH),
        pl.BlockSpec((None, B_HEADS, tm, 1), lambda bb, j: (bb, 0, j, 0)),
        pl.BlockSpec((None, B_HEADS, tm // ta, 1, ta), lambda bb, j: (bb, 0, j, 0, 0)),
        row_spec(A_WIDTH), row_spec(C_WIDTH),
        pl.BlockSpec((None, C_WIDTH, C_WIDTH), lambda bb, j: (bb, 0, 0)),
    ]
    return pl.pallas_call(
        functools.partial(_proj_prompt_kernel, tm=tm, ta=ta),
        grid=(b, nt),
        in_specs=in_specs,
        out_specs=out_specs,
        out_shape=out_shape,
        scratch_shapes=[
            pltpu.VMEM((C_WIDTH, C_WIDTH), F32),
            pltpu.VMEM((1, LANES), F32),
            pltpu.VMEM((B_HEADS, LANES), F32),
        ],
        compiler_params=pltpu.CompilerParams(
            dimension_semantics=("arbitrary", "arbitrary"), vmem_limit_bytes=VMEM_LIMIT),
        name="mix_proj_prompt",
    )(x3, mod, mod, g, w["wa"], w["wb"], w["wf"], w["wc"],
      w["gav"], w["gq"], w["gk"], w["bf"], w["gma"], w["gmc"], cst["cos_p"], cst["sin_p"],
      w["ws"], w["bsp_p"], cst["dintra_p"], cst["dq_p"], cst["dk_p"], cst["dch_p"],
      cst["seg"], cst["tril"], cst["triu"])


def _fox_prompt_kernel(q_ref, k_ref, v_ref, cc_ref, cr_ref, o_ref, *, ta):
    j = pl.program_id(2)
    row = lax.broadcasted_iota(jnp.int32, (ta, ta), 0)
    col = lax.broadcasted_iota(jnp.int32, (ta, ta), 1)
    causal = col <= row
    outs = []
    for i in range(2):
        lanes = slice(i * HEAD_DIM, (i + 1) * HEAD_DIM)
        qh = q_ref[:, lanes]
        cq = cc_ref[i]

        def block(kb, carry, masked, i=i, lanes=lanes, qh=qh, cq=cq):
            m, l, acc = carry
            start = pl.multiple_of(kb * ta, ta)
            kblk = k_ref[pl.ds(start, ta), lanes]
            vblk = v_ref[pl.ds(start, ta), lanes]
            s = _dot_nt(qh, kblk) + (cq - cr_ref[i, kb])
            if masked:
                s = jnp.where(causal, s, NEG_BIG)
            m_new = jnp.maximum(m, jnp.max(s, axis=-1, keepdims=True))
            p = jnp.exp(s - m_new)
            alpha = jnp.exp(m - m_new)
            l = alpha * l + jnp.sum(p, axis=-1, keepdims=True)
            acc = alpha * acc + _dot(p.astype(BF16), vblk)
            return m_new, l, acc

        init = (jnp.full((ta, 1), NEG_BIG, F32), jnp.zeros((ta, 1), F32), jnp.zeros((ta, HEAD_DIM), F32))
        carry = lax.fori_loop(0, j, functools.partial(block, masked=False), init)
        m, l, acc = block(j, carry, True)
        outs.append(acc / l)
    o_ref[...] = jnp.concatenate(outs, axis=-1)


def _fox_prompt_call(q16, k16, v16, cumc, cumr, *, ta):
    b, t, _ = q16.shape
    nhp = B_HEADS // 2
    return pl.pallas_call(
        functools.partial(_fox_prompt_kernel, ta=ta),
        grid=(b, nhp, t // ta),
        in_specs=[
            pl.BlockSpec((None, ta, LANES), lambda bb, hp, j: (bb, j, hp)),
            pl.BlockSpec((None, t, LANES), lambda bb, hp, j: (bb, 0, hp)),
            pl.BlockSpec((None, t, LANES), lambda bb, hp, j: (bb, 0, hp)),
            pl.BlockSpec((None, 2, ta, 1), lambda bb, hp, j: (bb, hp, j, 0)),
            pl.BlockSpec((None, 2, t // ta, 1, ta), lambda bb, hp, j: (bb, hp, 0, 0, 0)),
        ],
        out_specs=pl.BlockSpec((None, ta, LANES), lambda bb, hp, j: (bb, j, hp)),
        out_shape=jax.ShapeDtypeStruct((b, t, B_WIDTH), F32),
        compiler_params=pltpu.CompilerParams(
            dimension_semantics=("arbitrary", "arbitrary", "arbitrary"), vmem_limit_bytes=VMEM_LIMIT),
        name="fox_prompt",
    )(q16, k16, v16, cumc, cumr)


def _out_kernel(oa_ref, ob_ref, oc_ref, x_ref, gt_ref, gmb_ref, seg_ref, w_ref, o_ref):
    ob = (_head_norm(ob_ref[...], seg_ref[...]) * gmb_ref[...]).astype(BF16)
    ab = A_WIDTH + B_WIDTH
    y = (_dot(oa_ref[...], w_ref[:A_WIDTH, :]) + _dot(ob, w_ref[A_WIDTH:ab, :])
         + _dot(oc_ref[...], w_ref[ab:, :]))
    o_ref[...] = x_ref[...] + gt_ref[...] * y


def _out_call(oa, ob, oc, x2, mod, gmb, seg, w_out, *, per_row, tm, seq_len):
    n, d = x2.shape
    tiles_per_seq = seq_len // tm if not per_row else 1
    rows = lambda width: pl.BlockSpec((tm, width), lambda i: (i, 0))
    return pl.pallas_call(
        _out_kernel,
        grid=(n // tm,),
        in_specs=[
            rows(A_WIDTH), rows(B_WIDTH), rows(C_WIDTH), rows(d),
            _mod_specs(per_row, tm, d, tiles_per_seq, 1, 2),
            pl.BlockSpec((1, B_WIDTH), lambda i: (0, 0)),
            pl.BlockSpec((MXU_DIM, MXU_DIM), lambda i: (0, 0)),
            pl.BlockSpec((d, d), lambda i: (0, 0)),
        ],
        out_specs=rows(d),
        out_shape=jax.ShapeDtypeStruct((n, d), F32),
        compiler_params=pltpu.CompilerParams(
            dimension_semantics=("arbitrary",), vmem_limit_bytes=VMEM_LIMIT),
        name="merge_out",
    )(oa, ob, oc, x2, mod, gmb, seg, w_out)


def _proj_sample_kernel(
        x_ref, sh_ref, sc_ref, g_ref, wa_ref, wb_ref, wf_ref, wc_ref,
        gav_ref, gq_ref, gk_ref, bf_ref, gma_ref, gmc_ref, cos_ref, sin_ref,
        wst_ref, bsp_ref, dintra_ref, dq_ref, dk_ref, dch_ref, seg_ref, s0_ref,
        k32_ref, v32_ref, logf_ref, q32_ref, av_ref, oa_ref, oc_ref, sret_ref, *, ns, ts, db):
    x = x_ref[...]
    hb = _modnorm(x, g_ref[...], sh_ref[...], sc_ref[...]).astype(BF16)
    seg = seg_ref[...]

    za = _dot(hb, wa_ref[...])
    a_u = _gelu_tanh(za[:, :A_WIDTH])
    a_vr = _gelu_tanh(za[:, A_WIDTH:])
    a_v = a_vr * lax.rsqrt(jnp.mean(a_vr * a_vr, axis=-1, keepdims=True) + EPS) * gav_ref[...]
    av_ref[...] = a_v

    zb = _dot(hb, wb_ref[...])
    q32_ref[...] = _head_norm(zb[:, :B_WIDTH], seg) * gq_ref[...]
    k32_ref[...] = _head_norm(zb[:, B_WIDTH:2 * B_WIDTH], seg) * gk_ref[...]
    v32_ref[...] = zb[:, 2 * B_WIDTH:]
    logf = _log_sigmoid(_dot(hb, wf_ref[...]) + bf_ref[...])
    logf_ref[...] = logf[:, :B_HEADS]

    zc = _dot(hb, wc_ref[...])
    cos = cos_ref[...]
    sin = sin_ref[...]
    q_c = _rope(zc[:, :C_WIDTH], cos, sin)
    k_c = _rope(zc[:, C_WIDTH:2 * C_WIDTH], cos, sin) * (HEAD_DIM ** -0.5)
    v_c = zc[:, 2 * C_WIDTH:3 * C_WIDTH]
    c_g = zc[:, 3 * C_WIDTH:]

    row = lax.broadcasted_iota(jnp.int32, (ns, ns), 0)
    col = lax.broadcasted_iota(jnp.int32, (ns, ns), 1)
    keep = (row // ts == col // ts) & (col <= row)
    w_cat = jnp.concatenate(
        [jnp.where(keep, wst_ref[h], 0.0) for h in range(A_HEADS)], axis=1).astype(BF16)
    av_blk = _tile_rows_masked(a_v, A_HEADS, ns, HEAD_DIM).astype(BF16)
    o_a = a_u * (_dot(w_cat, av_blk) + bsp_ref[...])

    qc = q_c.astype(BF16)
    k_blk = _tile_rows_masked(k_c, C_HEADS, ns, HEAD_DIM).astype(BF16)
    v_blk = _tile_rows_masked(v_c, C_HEADS, ns, HEAD_DIM).astype(BF16)
    att = _dot_nt(qc, k_blk) * dintra_ref[...]
    intra = _dot(att.astype(BF16), v_blk)
    s0 = s0_ref[...]
    q_exp = _tile_cols_masked(q_c, db, ts, C_WIDTH).astype(BF16)
    inter = _dot(q_exp, s0.astype(BF16)) * dq_ref[...]
    o_c = intra + inter
    kd_t = jnp.transpose(k_c * dk_ref[...])
    kd_exp = _tile_rows_masked(kd_t, db, C_WIDTH, ts).astype(BF16)
    sret_ref[...] = s0 * dch_ref[...] + _dot(kd_exp, v_c.astype(BF16))

    oa_ref[...] = (_head_norm(o_a, seg) * gma_ref[...]).astype(BF16)
    oc_ref[...] = (_head_norm(o_c, seg) * gmc_ref[...] * _silu(c_g)).astype(BF16)


def _proj_sample_call(x2, mod_rows, g, w, cst, s0_blk, *, ts, db):
    ns, d = x2.shape
    in_arrays = [
        x2, mod_rows[:, 3 * d:4 * d], mod_rows[:, 4 * d:5 * d], g,
        w["wa"], w["wb"], w["wf"], w["wc"],
        w["gav"], w["gq"], w["gk"], w["bf"], w["gma"], w["gmc"], cst["cos_s"], cst["sin_s"],
        w["wst"], w["bsp_s"], cst["dintra_s"], cst["dq_s"], cst["dk_s"], cst["dch_s"], cst["seg"], s0_blk,
    ]
    out_shape = [
        jax.ShapeDtypeStruct((ns, B_WIDTH), F32), jax.ShapeDtypeStruct((ns, B_WIDTH), F32),
        jax.ShapeDtypeStruct((ns, B_HEADS), F32), jax.ShapeDtypeStruct((ns, B_WIDTH), F32),
        jax.ShapeDtypeStruct((ns, A_WIDTH), F32),
        jax.ShapeDtypeStruct((ns, A_WIDTH), BF16), jax.ShapeDtypeStruct((ns, C_WIDTH), BF16),
        jax.ShapeDtypeStruct(s0_blk.shape, F32),
    ]
    return pl.pallas_call(
        functools.partial(_proj_sample_kernel, ns=ns, ts=ts, db=db),
        out_shape=out_shape,
        compiler_params=pltpu.CompilerParams(vmem_limit_bytes=VMEM_LIMIT),
        name="mix_proj_sample",
    )(*in_arrays)


def _fox_sample_kernel(pt_ref, q_ref, kn_ref, vn_ref, lfn_ref, triu_ref, *rest, ts, gp, page):
    kp_refs = rest[:gp]
    vp_refs = rest[gp:2 * gp]
    lf_refs = rest[2 * gp:3 * gp]
    o_ref = rest[3 * gp]
    qb_scr, qf_scr, m_scr, l_scr, acc_scr, cr_scr = rest[3 * gp + 1:]
    del pt_ref
    pg = pl.program_id(1)
    nrow = ts * B_HEADS
    hmask = (lax.broadcasted_iota(jnp.int32, (B_HEADS, B_WIDTH), 1) // HEAD_DIM
             == lax.broadcasted_iota(jnp.int32, (B_HEADS, B_WIDTH), 0))

    @pl.when(pg == 0)
    def _():
        q = q_ref[...]
        for t in range(ts):
            qt = jnp.where(hmask, jnp.broadcast_to(q[t:t + 1, :], (B_HEADS, B_WIDTH)), 0.0)
            qf_scr[t * B_HEADS:(t + 1) * B_HEADS, :] = qt
            qb_scr[t * B_HEADS:(t + 1) * B_HEADS, :] = qt.astype(BF16)
        m_scr[...] = jnp.full(m_scr.shape, NEG_BIG, F32)
        l_scr[...] = jnp.zeros_like(l_scr)
        acc_scr[...] = jnp.zeros_like(acc_scr)
        cr_scr[...] = jnp.zeros_like(cr_scr)

    qb = qb_scr[...]
    triu = triu_ref[...]
    m = m_scr[...]
    l = l_scr[...]
    acc = acc_scr[...]
    carry = cr_scr[:, 0:1]
    for g in range(gp):
        cum_t = _dot_split_lhs(lf_refs[g][...], triu, 3) + carry
        carry = cum_t[:, page - 1:page]
        s = _dot_nt(qb, kp_refs[g][...]) - jnp.concatenate([cum_t] * ts, axis=0)
        m_new = jnp.maximum(m, jnp.max(s, axis=-1, keepdims=True))
        p = jnp.exp(s - m_new)
        alpha = jnp.exp(m - m_new)
        l = alpha * l + jnp.sum(p, axis=-1, keepdims=True)
        acc = alpha * acc + _dot(p.astype(BF16), vp_refs[g][...])
        m = m_new
    m_scr[...] = m
    l_scr[...] = l
    acc_scr[...] = acc
    cr_scr[...] = jnp.broadcast_to(carry, cr_scr.shape)

    @pl.when(pg == pl.num_programs(1) - 1)
    def _():
        qf = qf_scr[...]
        kn = kn_ref[...]
        vn = vn_ref[...]
        lfn = lfn_ref[...]
        tok = lax.broadcasted_iota(jnp.int32, (nrow, 1), 0) // B_HEADS
        run = carry
        s_new = []
        for jn in range(ts):
            run = run + lfn[:, jn:jn + 1]
            sj = jnp.sum(qf * kn[jn:jn + 1, :], axis=-1, keepdims=True) - jnp.concatenate([run] * ts, axis=0)
            s_new.append(jnp.where(tok >= jn, sj, NEG_BIG))
        m2 = m
        for sj in s_new:
            m2 = jnp.maximum(m2, sj)
        alpha = jnp.exp(m - m2)
        l2 = alpha * l
        acc2 = alpha * acc
        for jn in range(ts):
            pj = jnp.exp(s_new[jn] - m2)
            l2 = l2 + pj
            acc2 = acc2 + pj * vn[jn:jn + 1, :]
        o_full = acc2 / l2
        for t in range(ts):
            blk = jnp.where(hmask, o_full[t * B_HEADS:(t + 1) * B_HEADS, :], 0.0)
            o_ref[t:t + 1, :] = jnp.sum(blk, axis=0, keepdims=True)


def _fox_sample_call(page_table, q3, kn3, vn3, lfn_t, triu, kc, vc, lfc_t, layer, *, gp):
    db, ts, _ = q3.shape
    n_pages = page_table.shape[1]
    page = kc.shape[2]
    nrow = ts * B_HEADS
    seq3 = lambda shape: pl.BlockSpec((None,) + shape, lambda b, pg, pt: (b, 0, 0))

    def page_spec(shape, g):
        return pl.BlockSpec((None, None) + shape, lambda b, pg, pt, g=g: (layer, pt[b, pg * gp + g], 0, 0))

    in_specs = [seq3((ts, B_WIDTH)), seq3((ts, B_WIDTH)), seq3((ts, B_WIDTH)), seq3((B_HEADS, ts)),
                pl.BlockSpec((page, page), lambda b, pg, pt: (0, 0))]
    in_specs += [page_spec((page, B_WIDTH), g) for g in range(gp)]
    in_specs += [page_spec((page, B_WIDTH), g) for g in range(gp)]
    in_specs += [page_spec((B_HEADS, page), g) for g in range(gp)]
    grid_spec = pltpu.PrefetchScalarGridSpec(
        num_scalar_prefetch=1,
        grid=(db, n_pages // gp),
        in_specs=in_specs,
        out_specs=pl.BlockSpec((None, ts, B_WIDTH), lambda b, pg, pt: (b, 0, 0)),
        scratch_shapes=[
            pltpu.VMEM((nrow, B_WIDTH), BF16), pltpu.VMEM((nrow, B_WIDTH), F32),
            pltpu.VMEM((nrow, 1), F32), pltpu.VMEM((nrow, 1), F32),
            pltpu.VMEM((nrow, B_WIDTH), F32), pltpu.VMEM((B_HEADS, LANES), F32),
        ],
    )
    return pl.pallas_call(
        functools.partial(_fox_sample_kernel, ts=ts, gp=gp, page=page),
        grid_spec=grid_spec,
        out_shape=jax.ShapeDtypeStruct((db, ts, B_WIDTH), F32),
        compiler_params=pltpu.CompilerParams(
            dimension_semantics=("arbitrary", "arbitrary"), vmem_limit_bytes=VMEM_LIMIT),
        name="fox_sample",
    )(page_table, q3, kn3, vn3, lfn_t, triu, *([kc] * gp), *([vc] * gp), *([lfc_t] * gp))


def _rope_tables(pos):
    half = HEAD_DIM // 2
    inv = ROPE_BASE ** (-jnp.arange(half, dtype=F32) / half)
    ang = pos[:, None] * inv[None, :]
    cos = jnp.cos(ang)
    sin = jnp.sin(ang)
    cos_h = jnp.concatenate([cos, cos], axis=-1)
    sin_h = jnp.concatenate([-sin, sin], axis=-1)
    return jnp.tile(cos_h, (1, C_HEADS)), jnp.tile(sin_h, (1, C_HEADS))


def _decay_tables(length, log_gamma):
    idx = jnp.arange(length, dtype=F32)
    rel = idx[:, None] - idx[None, :]
    d_intra = jnp.where(rel[None] >= 0, jnp.exp(jnp.maximum(rel, 0.0)[None] * log_gamma[:, None, None]), 0.0)
    d_q = jnp.exp((idx + 1.0)[:, None] * log_gamma[None, :])
    d_k = jnp.exp((length - 1.0 - idx)[:, None] * log_gamma[None, :])
    d_chunk = jnp.exp(length * log_gamma)
    return d_intra, d_q, d_k, d_chunk


def _constants(t, ts, db, past, tm):
    log_gamma = jnp.log1p(-jnp.exp2(-5.0 - jnp.arange(C_HEADS, dtype=F32)))
    cst = {}
    cst["cos_p"], cst["sin_p"] = _rope_tables(jnp.arange(t, dtype=F32))
    cos_s, sin_s = _rope_tables(past + jnp.arange(ts, dtype=F32))
    cst["cos_s"], cst["sin_s"] = jnp.tile(cos_s, (db, 1)), jnp.tile(sin_s, (db, 1))
    rep = lambda a: jnp.repeat(a, HEAD_DIM, axis=-1)

    d_intra, d_q, d_k, d_chunk = _decay_tables(CHUNK, log_gamma)
    cst["dintra_p"] = jnp.concatenate([d_intra[h] for h in range(C_HEADS)], axis=1)
    cst["dq_p"], cst["dk_p"], cst["dch_p"] = rep(d_q), rep(d_k), rep(d_chunk[None, :])

    d_intra, d_q, d_k, d_chunk = _decay_tables(ts, log_gamma)
    same_seq = jnp.kron(jnp.eye(db, dtype=F32), jnp.ones((ts, ts), F32))
    cst["dintra_s"] = jnp.concatenate(
        [jnp.tile(d_intra[h], (db, db)) * same_seq for h in range(C_HEADS)], axis=1)
    cst["dq_s"] = jnp.tile(rep(d_q), (db, 1))
    cst["dk_s"] = jnp.tile(rep(d_k), (db, 1))
    cst["dch_s"] = rep(d_chunk[None, :])

    seg = jnp.kron(jnp.eye(MXU_DIM // HEAD_DIM, dtype=F32), jnp.ones((HEAD_DIM, HEAD_DIM), F32))
    cst["seg"] = seg.astype(BF16)
    cst["tril"] = jnp.tril(jnp.ones((tm, tm), F32)).astype(BF16)
    cst["triu"] = jnp.triu(jnp.ones((tm, tm), F32)).astype(BF16)
    return cst


def _layer_weights(l, w_in, g_a_v, w_spatial, b_spatial, g_qnorm, g_knorm, b_forget, g_mix, ts, db):
    d = w_in.shape[1]
    wl = w_in[l]
    o = 0
    wa = wl[:, o:o + 2 * A_WIDTH]; o += 2 * A_WIDTH
    wb = wl[:, o:o + 3 * B_WIDTH]; o += 3 * B_WIDTH
    wf = wl[:, o:o + B_HEADS]; o += B_HEADS
    wc = wl[:, o:o + 4 * C_WIDTH]
    w = {
        "wa": wa.astype(BF16), "wb": wb.astype(BF16), "wc": wc.astype(BF16),
        "wf": jnp.pad(wf, ((0, 0), (0, LANES - B_HEADS))).astype(BF16),
        "gav": g_a_v[l][None, :],
        "gq": jnp.tile(g_qnorm[l], B_HEADS)[None, :] * (HEAD_DIM ** -0.5),
        "gk": jnp.tile(g_knorm[l], B_HEADS)[None, :],
        "bf": jnp.pad(b_forget[l], (0, LANES - B_HEADS))[None, :],
        "gma": g_mix[l][None, :A_WIDTH],
        "gmb": g_mix[l][None, A_WIDTH:A_WIDTH + B_WIDTH],
        "gmc": g_mix[l][None, A_WIDTH + B_WIDTH:],
        "ws": w_spatial[l],
        "bsp_p": jnp.repeat(b_spatial[l].T, HEAD_DIM, axis=-1),
        "wst": jnp.tile(w_spatial[l][:, :ts, :ts], (1, db, db)),
        "bsp_s": jnp.tile(jnp.repeat(b_spatial[l][:, :ts].T, HEAD_DIM, axis=-1), (db, 1)),
    }
    del d
    return w


def kernel(x_prompt, x_sample, cache_k, cache_v, cache_logf, state_ret, page_table, c_prompt, c_sample,
           g_norm, w_ada, b_ada, w_ffn_gate, w_ffn_up, w_ffn_down, w_in, g_a_v, w_spatial, b_spatial,
           g_qnorm, g_knorm, b_forget, g_mix, w_out):
    b, t, d = x_prompt.shape
    db, ts, _ = x_sample.shape
    depth = w_in.shape[0]
    ns = db * ts
    n_pool, page = cache_k.shape[1], cache_k.shape[2]
    n_pages = page_table.shape[1]
    past = n_pages * page
    d_ff = w_ffn_gate.shape[-1]

    tm_ffn = min(512, t)
    tm_mix = min(512, t)
    ta = min(256, t)
    nf = 2
    tf = d_ff // nf
    gp = min(8, n_pages)

    cst = _constants(t, ts, db, past, tm_mix)
    triu_page = jnp.triu(jnp.ones((page, page), F32)).astype(BF16)

    mod = _ada_call(jnp.concatenate([c_prompt, c_sample], axis=0), w_ada, b_ada)

    kc = cache_k.reshape(depth, n_pool, page, B_WIDTH).astype(BF16)
    vc = cache_v.reshape(depth, n_pool, page, B_WIDTH).astype(BF16)
    lfc_t = jnp.swapaxes(cache_logf, 2, 3)

    x_p = x_prompt.reshape(b * t, d)
    x_s = x_sample.reshape(ns, d)
    outs = {k: [] for k in ("kp", "vp", "lp", "rp", "ks", "vs", "ls", "rs", "av")}
    head_idx = jnp.arange(C_HEADS)
    for l in range(depth):
        mod_p = mod[l, :b].reshape(b, 1, -1)
        mod_s = jnp.repeat(mod[l, b:], ts, axis=0)
        w = _layer_weights(l, w_in, g_a_v, w_spatial, b_spatial, g_qnorm, g_knorm, b_forget, g_mix, ts, db)
        ffn_w = []
        for i in range(2):
            wg3 = w_ffn_gate[l, i].astype(BF16).reshape(d, nf, tf).transpose(1, 0, 2)
            wu3 = w_ffn_up[l, i].astype(BF16).reshape(d, nf, tf).transpose(1, 0, 2)
            wd3 = w_ffn_down[l, i].astype(BF16).reshape(nf, tf, d)
            ffn_w.append((wg3, wu3, wd3))
        w_out_l = w_out[l].astype(BF16)

        x_p = _ffn_call(x_p, mod_p, 0, g_norm[l, 0][None, :], *ffn_w[0], per_row=False, tm=tm_ffn, seq_len=t)
        x_s = _ffn_call(x_s, mod_s, 0, g_norm[l, 0][None, :], *ffn_w[0], per_row=True, tm=ns, seq_len=ts)

        (k32, v32, logf, q16, k16, v16, cumc, cumr, oa, oc, sret) = _proj_prompt_call(
            x_p.reshape(b, t, d), mod_p, g_norm[l, 1][None, :], w, cst, tm=tm_mix, ta=ta)
        ob = _fox_prompt_call(q16, k16, v16, cumc, cumr, ta=ta)
        x_p = _out_call(oa.reshape(b * t, -1), ob.reshape(b * t, -1), oc.reshape(b * t, -1), x_p, mod_p,
                        w["gmb"], cst["seg"], w_out_l, per_row=False, tm=tm_mix, seq_len=t)
        outs["kp"].append(k32.reshape(b, t, B_HEADS, HEAD_DIM))
        outs["vp"].append(v32.reshape(b, t, B_HEADS, HEAD_DIM))
        outs["lp"].append(logf)
        s5 = sret.reshape(b, C_HEADS, HEAD_DIM, C_HEADS, HEAD_DIM)
        outs["rp"].append(jnp.moveaxis(s5[:, head_idx, :, head_idx, :], 0, 1))

        s0_blk = jnp.einsum("bhde,hg->bhdge", state_ret[l], jnp.eye(C_HEADS, dtype=F32)).reshape(
            db * C_WIDTH, C_WIDTH)
        (k32s, v32s, logfs, q32s, avs, oas, ocs, srets) = _proj_sample_call(
            x_s, mod_s, g_norm[l, 1][None, :], w, cst, s0_blk, ts=ts, db=db)
        obs = _fox_sample_call(
            page_table, q32s.reshape(db, ts, -1), k32s.reshape(db, ts, -1), v32s.reshape(db, ts, -1),
            jnp.swapaxes(logfs.reshape(db, ts, B_HEADS), 1, 2), triu_page, kc, vc, lfc_t, l, gp=gp)
        x_s = _out_call(oas, obs.reshape(ns, -1), ocs, x_s, mod_s,
                        w["gmb"], cst["seg"], w_out_l, per_row=True, tm=ns, seq_len=ts)
        outs["ks"].append(k32s.reshape(db, ts, B_HEADS, HEAD_DIM))
        outs["vs"].append(v32s.reshape(db, ts, B_HEADS, HEAD_DIM))
        outs["ls"].append(logfs.reshape(db, ts, B_HEADS))
        s5 = srets.reshape(db, C_HEADS, HEAD_DIM, C_HEADS, HEAD_DIM)
        outs["rs"].append(jnp.moveaxis(s5[:, head_idx, :, head_idx, :], 0, 1))
        outs["av"].append(avs.reshape(db, ts, A_WIDTH))

        x_p = _ffn_call(x_p, mod_p, 2, g_norm[l, 2][None, :], *ffn_w[1], per_row=False, tm=tm_ffn, seq_len=t)
        x_s = _ffn_call(x_s, mod_s, 2, g_norm[l, 2][None, :], *ffn_w[1], per_row=True, tm=ns, seq_len=ts)

    st = lambda key: jnp.stack(outs[key])
    return (x_p.reshape(b, t, d), x_s.reshape(db, ts, d), st("kp"), st("vp"), st("lp"), st("rp"),
            st("ks"), st("vs"), st("ls"), st("rs"), st("av"))
```

```python
import functools

import numpy as np
import jax
import jax.numpy as jnp
from jax import lax
from jax.experimental import pallas as pl
from jax.experimental.pallas import tpu as pltpu

F32 = jnp.float32
BF16 = jnp.bfloat16

HEAD_DIM = 64
A_HEADS, B_HEADS, C_HEADS = 4, 8, 4
A_WIDTH, B_WIDTH, C_WIDTH = A_HEADS * HEAD_DIM, B_HEADS * HEAD_DIM, C_HEADS * HEAD_DIM
CHUNK = 128
ROPE_BASE = 10000.0
EPS = 1e-6
N_SUBLAYERS = 3
LANES = 128
MXU_DIM = 256
NEG_BIG = -1e30
QSPLIT = 512
VMEM_LIMIT = 56 * 1024 * 1024


def _dot(a, b):
    return jnp.dot(a, b, preferred_element_type=F32)


def _dot_nt(a, b):
    return lax.dot_general(a, b, (((1,), (1,)), ((), ())), preferred_element_type=F32)


def _silu(x):
    return x / (1.0 + jnp.exp(-x))


def _gelu_tanh(x):
    c = 0.7978845608028654
    return 0.5 * x * (1.0 + jnp.tanh(c * (x + 0.044715 * (x * x * x))))


def _log_sigmoid(x):
    return jnp.minimum(x, 0.0) - jnp.log1p(jnp.exp(-jnp.abs(x)))


def _modnorm(x, g, shift, scale):
    ms = jnp.mean(x * x, axis=-1, keepdims=True)
    return (x * lax.rsqrt(ms + EPS) * g) * (1.0 + scale) + shift


def _split_bf16(x, parts):
    out = []
    r = x
    for i in range(parts):
        p = r.astype(BF16)
        out.append(p)
        if i + 1 < parts:
            r = r - p.astype(F32)
    return out


def _dot_split_lhs(a, b_bf16, parts):
    acc = None
    for p in _split_bf16(a, parts):
        t = _dot(p, b_bf16)
        acc = t if acc is None else acc + t
    return acc


def _dot_split_rhs(a_bf16, b, parts):
    acc = None
    for p in _split_bf16(b, parts):
        t = _dot(a_bf16, p)
        acc = t if acc is None else acc + t
    return acc


def _head_mean_sq(x, seg):
    w = x.shape[-1]
    outs = []
    for c in range(w // MXU_DIM):
        xs = x[:, c * MXU_DIM:(c + 1) * MXU_DIM]
        outs.append(_dot_split_lhs(xs * xs, seg, 2))
    ss = outs[0] if len(outs) == 1 else jnp.concatenate(outs, axis=-1)
    return ss * (1.0 / HEAD_DIM)


def _head_norm(x, seg):
    return x * lax.rsqrt(_head_mean_sq(x, seg) + EPS)


def _rope(x, cos, sin_signed):
    outs = []
    for c in range(x.shape[-1] // LANES):
        xs = x[:, c * LANES:(c + 1) * LANES]
        fwd = pltpu.roll(xs, HEAD_DIM // 2, axis=1)
        bwd = pltpu.roll(xs, LANES - HEAD_DIM // 2, axis=1)
        lane = lax.broadcasted_iota(jnp.int32, xs.shape, 1)
        outs.append(jnp.where((lane % HEAD_DIM) < HEAD_DIM // 2, bwd, fwd))
    partner = jnp.concatenate(outs, axis=-1)
    return x * cos + partner * sin_signed


def _tile_rows_masked(x, nrep, rows_per, cols_per):
    xt = jnp.concatenate([x] * nrep, axis=0)
    r = lax.broadcasted_iota(jnp.int32, xt.shape, 0) // rows_per
    c = lax.broadcasted_iota(jnp.int32, xt.shape, 1) // cols_per
    return jnp.where(r == c, xt, 0.0)


def _tile_cols_masked(x, nrep, rows_per, cols_per):
    xt = jnp.concatenate([x] * nrep, axis=1)
    r = lax.broadcasted_iota(jnp.int32, xt.shape, 0) // rows_per
    c = lax.broadcasted_iota(jnp.int32, xt.shape, 1) // cols_per
    return jnp.where(r == c, xt, 0.0)


def _ada_kernel(c_ref, w_ref, b_ref, o_ref):
    a = _silu(c_ref[...]).astype(BF16)
    o_ref[...] = _dot(a, w_ref[...].astype(BF16)) + b_ref[...]


def _ada_call(c_all, w_ada, b_ada):
    depth, d, ncol = w_ada.shape
    rows = c_all.shape[0]
    tn = 1024
    return pl.pallas_call(
        _ada_kernel,
        grid=(depth, ncol // tn),
        in_specs=[
            pl.BlockSpec((rows, d), lambda l, n: (0, 0)),
            pl.BlockSpec((None, d, tn), lambda l, n: (l, 0, n)),
            pl.BlockSpec((None, 1, tn), lambda l, n: (l, 0, n)),
        ],
        out_specs=pl.BlockSpec((None, rows, tn), lambda l, n: (l, 0, n)),
        out_shape=jax.ShapeDtypeStruct((depth, rows, ncol), F32),
        compiler_params=pltpu.CompilerParams(
            dimension_semantics=("arbitrary", "arbitrary"), vmem_limit_bytes=VMEM_LIMIT),
        name="adaln",
    )(c_all, w_ada, b_ada.reshape(depth, 1, ncol))


def _mod_specs(per_row, tm, d, tiles_per_seq, sub, which):
    col = sub * 3 + which
    if per_row:
        return pl.BlockSpec((tm, d), lambda i: (i, col))
    return pl.BlockSpec((None, 1, d), lambda i: (i // tiles_per_seq, 0, col))


def _ffn_kernel(x_ref, sh_ref, sc_ref, gt_ref, g_ref, wg_ref, wu_ref, wd_ref, o_ref, *, nf):
    x = x_ref[...]
    h = _modnorm(x, g_ref[...], sh_ref[...], sc_ref[...]).astype(BF16)
    acc = None
    for f in range(nf):
        gate = _dot(h, wg_ref[f])
        up = _dot(h, wu_ref[f])
        a = (_silu(gate) * up).astype(BF16)
        t = _dot(a, wd_ref[f])
        acc = t if acc is None else acc + t
    o_ref[...] = x + (0.5 * gt_ref[...]) * acc


def _ffn_call(x2, mod, sub, g, wg3, wu3, wd3, *, per_row, tm, seq_len):
    n, d = x2.shape
    nf, _, tf = wg3.shape
    tiles_per_seq = seq_len // tm if not per_row else 1
    const3 = lambda i: (0, 0, 0)
    return pl.pallas_call(
        functools.partial(_ffn_kernel, nf=nf),
        grid=(n // tm,),
        in_specs=[
            pl.BlockSpec((tm, d), lambda i: (i, 0)),
            _mod_specs(per_row, tm, d, tiles_per_seq, sub, 0),
            _mod_specs(per_row, tm, d, tiles_per_seq, sub, 1),
            _mod_specs(per_row, tm, d, tiles_per_seq, sub, 2),
            pl.BlockSpec((1, d), lambda i: (0, 0)),
            pl.BlockSpec((nf, d, tf), const3, pipeline_mode=pl.Buffered(1)),
            pl.BlockSpec((nf, d, tf), const3, pipeline_mode=pl.Buffered(1)),
            pl.BlockSpec((nf, tf, d), const3, pipeline_mode=pl.Buffered(1)),
        ],
        out_specs=pl.BlockSpec((tm, d), lambda i: (i, 0)),
        out_shape=jax.ShapeDtypeStruct((n, d), F32),
        compiler_params=pltpu.CompilerParams(
            dimension_semantics=("arbitrary",), vmem_limit_bytes=VMEM_LIMIT),
        name="half_ffn",
    )(x2, mod, mod, mod, g, wg3, wu3, wd3)


def _proj_prompt_kernel(
        x_ref, sh_ref, sc_ref, g_ref, wa_ref, wb_ref, wf_ref, wc_ref,
        gav_ref, gq_ref, gk_ref, bf_ref, gma_ref, gmc_ref, cos_ref, sin_ref,
        ws_ref, bsp_ref, dintra_ref, dq_ref, dk_ref, dch_ref, seg_ref, tril_ref,
        pk_ref, kconst_ref, pq_ref, qconst_ref,
        kt32_ref, vt32_ref, logf_ref, qaug_ref, kaug_ref, vt16_ref,
        oa_ref, oc_ref, sret_ref,
        s_scr, cc_scr, *, tm, ta):
    j = pl.program_id(1)

    @pl.when(j == 0)
    def _():
        s_scr[...] = jnp.zeros_like(s_scr)
        cc_scr[...] = jnp.zeros_like(cc_scr)

    x = x_ref[...]
    hb = _modnorm(x, g_ref[...], sh_ref[...], sc_ref[...]).astype(BF16)
    seg = seg_ref[...]

    za = _dot(hb, wa_ref[...])
    a_u = _gelu_tanh(za[:, :A_WIDTH])
    a_vr = _gelu_tanh(za[:, A_WIDTH:])
    a_v = a_vr * lax.rsqrt(jnp.mean(a_vr * a_vr, axis=-1, keepdims=True) + EPS) * gav_ref[...]

    zb = _dot(hb, wb_ref[...])
    q_b = _head_norm(zb[:, :B_WIDTH], seg) * gq_ref[...]
    k_b = _head_norm(zb[:, B_WIDTH:2 * B_WIDTH], seg) * gk_ref[...]
    v_b = zb[:, 2 * B_WIDTH:]
    kt32_ref[...] = jnp.transpose(k_b)
    v_t = jnp.transpose(v_b)
    vt32_ref[...] = v_t
    v_t16 = v_t.astype(BF16)
    for hp in range(B_HEADS // 2):
        for c in range(tm // ta):
            vt16_ref[hp, c] = v_t16[hp * LANES:(hp + 1) * LANES, c * ta:(c + 1) * ta]

    logf = _log_sigmoid(_dot(hb, wf_ref[...]) + bf_ref[...])
    logf_ref[...] = logf[:, :B_HEADS]
    cum_c = _dot_split_rhs(tril_ref[...], logf, 3) + cc_scr[...]
    cc_scr[...] = cum_c[tm - 1:tm, :]
    pieces = jnp.concatenate(_split_bf16(cum_c, 3), axis=-1)
    k_aux = (_dot(pieces, pk_ref[...]) + kconst_ref[...]).astype(BF16)
    q_aux = (_dot(pieces, pq_ref[...]) + qconst_ref[...]).astype(BF16)
    q16 = q_b.astype(BF16)
    k16 = k_b.astype(BF16)
    lane = lax.broadcasted_iota(jnp.int32, (tm, LANES), 1)
    for hp in range(B_HEADS // 2):
        pair = slice(hp * LANES, (hp + 1) * LANES)
        kaug_ref[hp, :, :LANES] = k16[:, pair]
        kaug_ref[hp, :, LANES:] = k_aux[:, pair]
        for i in range(2):
            h = 2 * hp + i
            own = (lane < HEAD_DIM) if i == 0 else (lane >= HEAD_DIM)
            qaug_ref[h, :, :LANES] = jnp.where(own, q16[:, pair], jnp.zeros_like(q16[:, pair]))
            qaug_ref[h, :, LANES:] = q_aux[:, h * LANES:(h + 1) * LANES]

    zc = _dot(hb, wc_ref[...])
    cos = cos_ref[...]
    sin = sin_ref[...]
    q_c = _rope(zc[:, :C_WIDTH], cos, sin)
    k_c = _rope(zc[:, C_WIDTH:2 * C_WIDTH], cos, sin) * (HEAD_DIM ** -0.5)
    v_c = zc[:, 2 * C_WIDTH:3 * C_WIDTH]
    c_g = zc[:, 3 * C_WIDTH:]

    row = lax.broadcasted_iota(jnp.int32, (CHUNK, CHUNK), 0)
    col = lax.broadcasted_iota(jnp.int32, (CHUNK, CHUNK), 1)
    w_cat = jnp.concatenate(
        [jnp.where(col <= row, ws_ref[h], 0.0) for h in range(A_HEADS)], axis=1).astype(BF16)
    dintra = dintra_ref[...]
    dq = dq_ref[...]
    dk = dk_ref[...]
    dch = dch_ref[...]
    bsp = bsp_ref[...]
    sr = lax.broadcasted_iota(jnp.int32, (C_WIDTH, C_WIDTH), 0) // HEAD_DIM
    scol = lax.broadcasted_iota(jnp.int32, (C_WIDTH, C_WIDTH), 1) // HEAD_DIM
    s_mask = sr == scol
    s_blk = s_scr[...]
    oa_parts = []
    oc_parts = []
    for c in range(tm // CHUNK):
        sl = slice(c * CHUNK, (c + 1) * CHUNK)
        av_blk = _tile_rows_masked(a_v[sl], A_HEADS, CHUNK, HEAD_DIM).astype(BF16)
        oa_parts.append(a_u[sl] * (_dot(w_cat, av_blk) + bsp))

        qc = q_c[sl].astype(BF16)
        k_blk = _tile_rows_masked(k_c[sl], C_HEADS, CHUNK, HEAD_DIM).astype(BF16)
        v_blk = _tile_rows_masked(v_c[sl], C_HEADS, CHUNK, HEAD_DIM).astype(BF16)
        att = _dot_nt(qc, k_blk) * dintra
        intra = _dot(att.astype(BF16), v_blk)
        inter = _dot(qc, s_blk.astype(BF16)) * dq
        oc_parts.append(intra + inter)
        kd_t = jnp.transpose(k_c[sl] * dk).astype(BF16)
        upd = _dot(kd_t, v_c[sl].astype(BF16))
        s_blk = s_blk * dch + jnp.where(s_mask, upd, 0.0)
    s_scr[...] = s_blk
    sret_ref[...] = s_blk

    o_a = oa_parts[0] if len(oa_parts) == 1 else jnp.concatenate(oa_parts, axis=0)
    o_c = oc_parts[0] if len(oc_parts) == 1 else jnp.concatenate(oc_parts, axis=0)
    oa_ref[...] = (_head_norm(o_a, seg) * gma_ref[...]).astype(BF16)
    oc_ref[...] = (_head_norm(o_c, seg) * gmc_ref[...] * _silu(c_g)).astype(BF16)


def _proj_prompt_call(x3, mod, g, w, cst, *, tm, ta):
    b, t, d = x3.shape
    nt = t // tm
    full2 = lambda bb, j: (0, 0)
    full3 = lambda bb, j: (0, 0, 0)
    row_spec = lambda width: pl.BlockSpec((None, tm, width), lambda bb, j: (bb, j, 0))
    mod_spec = lambda col: pl.BlockSpec((None, 1, d), lambda bb, j: (bb, 0, col))
    vec = lambda width: pl.BlockSpec((1, width), full2)
    in_specs = [
        row_spec(d), mod_spec(3), mod_spec(4), vec(d),
        pl.BlockSpec((d, 2 * A_WIDTH), full2), pl.BlockSpec((d, 3 * B_WIDTH), full2),
        pl.BlockSpec((d, LANES), full2), pl.BlockSpec((d, 4 * C_WIDTH), full2),
        vec(A_WIDTH), vec(B_WIDTH), vec(B_WIDTH), vec(LANES), vec(A_WIDTH), vec(C_WIDTH),
        pl.BlockSpec((tm, C_WIDTH), lambda bb, j: (j, 0)),
        pl.BlockSpec((tm, C_WIDTH), lambda bb, j: (j, 0)),
        pl.BlockSpec((A_HEADS, CHUNK, CHUNK), full3),
        pl.BlockSpec((CHUNK, A_WIDTH), full2),
        pl.BlockSpec((CHUNK, C_HEADS * CHUNK), full2),
        pl.BlockSpec((CHUNK, C_WIDTH), full2), pl.BlockSpec((CHUNK, C_WIDTH), full2),
        vec(C_WIDTH),
        pl.BlockSpec((MXU_DIM, MXU_DIM), full2),
        pl.BlockSpec((tm, tm), full2),
        pl.BlockSpec((3 * LANES, (B_HEADS // 2) * LANES), full2), vec((B_HEADS // 2) * LANES),
        pl.BlockSpec((3 * LANES, B_HEADS * LANES), full2), vec(B_HEADS * LANES),
    ]
    nhp = B_HEADS // 2
    col_spec = pl.BlockSpec((None, B_WIDTH, tm), lambda bb, j: (bb, 0, j))
    out_shape = [
        jax.ShapeDtypeStruct((b, B_WIDTH, t), F32), jax.ShapeDtypeStruct((b, B_WIDTH, t), F32),
        jax.ShapeDtypeStruct((b, t, B_HEADS), F32),
        jax.ShapeDtypeStruct((b, B_HEADS, t, 2 * LANES), BF16),
        jax.ShapeDtypeStruct((b, nhp, t, 2 * LANES), BF16),
        jax.ShapeDtypeStruct((b, nhp, t // ta, LANES, ta), BF16),
        jax.ShapeDtypeStruct((b, t, A_WIDTH), BF16), jax.ShapeDtypeStruct((b, t, C_WIDTH), BF16),
        jax.ShapeDtypeStruct((b, C_WIDTH, C_WIDTH), F32),
    ]
    out_specs = [
        col_spec, col_spec, row_spec(B_HEADS),
        pl.BlockSpec((None, B_HEADS, tm, 2 * LANES), lambda bb, j: (bb, 0, j, 0)),
        pl.BlockSpec((None, nhp, tm, 2 * LANES), lambda bb, j: (bb, 0, j, 0)),
        pl.BlockSpec((None, nhp, tm // ta, LANES, ta), lambda bb, j: (bb, 0, j, 0, 0)),
        row_spec(A_WIDTH), row_spec(C_WIDTH),
        pl.BlockSpec((None, C_WIDTH, C_WIDTH), lambda bb, j: (bb, 0, 0)),
    ]
    return pl.pallas_call(
        functools.partial(_proj_prompt_kernel, tm=tm, ta=ta),
        grid=(b, nt),
        in_specs=in_specs,
        out_specs=out_specs,
        out_shape=out_shape,
        scratch_shapes=[
            pltpu.VMEM((C_WIDTH, C_WIDTH), F32),
            pltpu.VMEM((1, LANES), F32),
        ],
        compiler_params=pltpu.CompilerParams(
            dimension_semantics=("arbitrary", "arbitrary"), vmem_limit_bytes=VMEM_LIMIT),
        name="mix_proj_prompt",
    )(x3, mod, mod, g, w["wa"], w["wb"], w["wf"], w["wc"],
      w["gav"], w["gq"], w["gk"], w["bf"], w["gma"], w["gmc"], cst["cos_p"], cst["sin_p"],
      w["ws"], w["bsp_p"], cst["dintra_p"], cst["dq_p"], cst["dk_p"], cst["dch_p"],
      cst["seg"], cst["tril"], cst["pk"], cst["kconst"], cst["pq"], cst["qconst"])


def _fox_prompt_kernel(q_ref, k_ref, vt_ref, o_ref, *, ta):
    j = pl.program_id(2)
    nq = ta // QSPLIT
    q_parts = [q_ref[i, g * QSPLIT:(g + 1) * QSPLIT, :] for i in range(2) for g in range(nq)]
    kpos = lax.broadcasted_iota(jnp.int32, (ta, QSPLIT), 0)
    qpos = lax.broadcasted_iota(jnp.int32, (ta, QSPLIT), 1)

    def step(kb, carry, masked):
        kblk = k_ref[pl.ds(pl.multiple_of(kb * ta, ta), ta), :]
        vblk = vt_ref[kb]
        new = []
        for c, (m, l, acc) in enumerate(carry):
            i, g = divmod(c, nq)
            s = _dot_nt(kblk, q_parts[c])
            if masked:
                s = jnp.where(kpos <= qpos + g * QSPLIT, s, NEG_BIG)
            m_new = jnp.maximum(m, jnp.max(s, axis=0, keepdims=True))
            p = jnp.exp(s - m_new)
            alpha = jnp.exp(m - m_new)
            l = alpha * l + jnp.sum(p, axis=0, keepdims=True)
            r = _dot(vblk, p.astype(BF16))
            acc = alpha * acc + r[i * HEAD_DIM:(i + 1) * HEAD_DIM]
            new.append((m_new, l, acc))
        return tuple(new)

    init = tuple((jnp.full((1, QSPLIT), NEG_BIG, F32), jnp.zeros((1, QSPLIT), F32),
                  jnp.zeros((HEAD_DIM, QSPLIT), F32)) for _ in range(2 * nq))
    carry = lax.fori_loop(0, j, functools.partial(step, masked=False), init)
    carry = step(j, carry, True)
    heads = [jnp.concatenate([acc / l for (_, l, acc) in carry[i * nq:(i + 1) * nq]], axis=1) for i in range(2)]
    o_ref[...] = jnp.transpose(jnp.concatenate(heads, axis=0))


def _fox_prompt_call(qaug, kaug, vt16, *, ta):
    b, _, t, wq = qaug.shape
    nhp = B_HEADS // 2
    return pl.pallas_call(
        functools.partial(_fox_prompt_kernel, ta=ta),
        grid=(b, nhp, t // ta),
        in_specs=[
            pl.BlockSpec((None, 2, ta, wq), lambda bb, hp, j: (bb, hp, j, 0)),
            pl.BlockSpec((None, None, t, wq), lambda bb, hp, j: (bb, hp, 0, 0)),
            pl.BlockSpec((None, None, t // ta, LANES, ta), lambda bb, hp, j: (bb, hp, 0, 0, 0)),
        ],
        out_specs=pl.BlockSpec((None, ta, LANES), lambda bb, hp, j: (bb, j, hp)),
        out_shape=jax.ShapeDtypeStruct((b, t, B_WIDTH), F32),
        compiler_params=pltpu.CompilerParams(
            dimension_semantics=("arbitrary", "arbitrary", "arbitrary"), vmem_limit_bytes=VMEM_LIMIT),
        name="fox_prompt",
    )(qaug, kaug, vt16)


def _out_kernel(oa_ref, ob_ref, oc_ref, x_ref, gt_ref, gmb_ref, seg_ref, w_ref, o_ref):
    ob = (_head_norm(ob_ref[...], seg_ref[...]) * gmb_ref[...]).astype(BF16)
    ab = A_WIDTH + B_WIDTH
    y = (_dot(oa_ref[...], w_ref[:A_WIDTH, :]) + _dot(ob, w_ref[A_WIDTH:ab, :])
         + _dot(oc_ref[...], w_ref[ab:, :]))
    o_ref[...] = x_ref[...] + gt_ref[...] * y


def _out_call(oa, ob, oc, x2, mod, gmb, seg, w_out, *, per_row, tm, seq_len):
    n, d = x2.shape
    tiles_per_seq = seq_len // tm if not per_row else 1
    rows = lambda width: pl.BlockSpec((tm, width), lambda i: (i, 0))
    return pl.pallas_call(
        _out_kernel,
        grid=(n // tm,),
        in_specs=[
            rows(A_WIDTH), rows(B_WIDTH), rows(C_WIDTH), rows(d),
            _mod_specs(per_row, tm, d, tiles_per_seq, 1, 2),
            pl.BlockSpec((1, B_WIDTH), lambda i: (0, 0)),
            pl.BlockSpec((MXU_DIM, MXU_DIM), lambda i: (0, 0)),
            pl.BlockSpec((d, d), lambda i: (0, 0)),
        ],
        out_specs=rows(d),
        out_shape=jax.ShapeDtypeStruct((n, d), F32),
        compiler_params=pltpu.CompilerParams(
            dimension_semantics=("arbitrary",), vmem_limit_bytes=VMEM_LIMIT),
        name="merge_out",
    )(oa, ob, oc, x2, mod, gmb, seg, w_out)


def _proj_sample_kernel(
        x_ref, sh_ref, sc_ref, g_ref, wa_ref, wb_ref, wf_ref, wc_ref,
        gav_ref, gq_ref, gk_ref, bf_ref, gma_ref, gmc_ref, cos_ref, sin_ref,
        wst_ref, bsp_ref, dintra_ref, dq_ref, dk_ref, dch_ref, seg_ref, s0_ref,
        k32_ref, v32_ref, logf_ref, q32_ref, av_ref, oa_ref, oc_ref, sret_ref, *, ns, ts, db):
    x = x_ref[...]
    hb = _modnorm(x, g_ref[...], sh_ref[...], sc_ref[...]).astype(BF16)
    seg = seg_ref[...]

    za = _dot(hb, wa_ref[...])
    a_u = _gelu_tanh(za[:, :A_WIDTH])
    a_vr = _gelu_tanh(za[:, A_WIDTH:])
    a_v = a_vr * lax.rsqrt(jnp.mean(a_vr * a_vr, axis=-1, keepdims=True) + EPS) * gav_ref[...]
    av_ref[...] = a_v

    zb = _dot(hb, wb_ref[...])
    q32_ref[...] = _head_norm(zb[:, :B_WIDTH], seg) * gq_ref[...]
    k32_ref[...] = _head_norm(zb[:, B_WIDTH:2 * B_WIDTH], seg) * gk_ref[...]
    v32_ref[...] = zb[:, 2 * B_WIDTH:]
    logf = _log_sigmoid(_dot(hb, wf_ref[...]) + bf_ref[...])
    logf_ref[...] = logf[:, :B_HEADS]

    zc = _dot(hb, wc_ref[...])
    cos = cos_ref[...]
    sin = sin_ref[...]
    q_c = _rope(zc[:, :C_WIDTH], cos, sin)
    k_c = _rope(zc[:, C_WIDTH:2 * C_WIDTH], cos, sin) * (HEAD_DIM ** -0.5)
    v_c = zc[:, 2 * C_WIDTH:3 * C_WIDTH]
    c_g = zc[:, 3 * C_WIDTH:]

    row = lax.broadcasted_iota(jnp.int32, (ns, ns), 0)
    col = lax.broadcasted_iota(jnp.int32, (ns, ns), 1)
    keep = (row // ts == col // ts) & (col <= row)
    w_cat = jnp.concatenate(
        [jnp.where(keep, wst_ref[h], 0.0) for h in range(A_HEADS)], axis=1).astype(BF16)
    av_blk = _tile_rows_masked(a_v, A_HEADS, ns, HEAD_DIM).astype(BF16)
    o_a = a_u * (_dot(w_cat, av_blk) + bsp_ref[...])

    qc = q_c.astype(BF16)
    k_blk = _tile_rows_masked(k_c, C_HEADS, ns, HEAD_DIM).astype(BF16)
    v_blk = _tile_rows_masked(v_c, C_HEADS, ns, HEAD_DIM).astype(BF16)
    att = _dot_nt(qc, k_blk) * dintra_ref[...]
    intra = _dot(att.astype(BF16), v_blk)
    s0 = s0_ref[...]
    q_exp = _tile_cols_masked(q_c, db, ts, C_WIDTH).astype(BF16)
    inter = _dot(q_exp, s0.astype(BF16)) * dq_ref[...]
    o_c = intra + inter
    kd_t = jnp.transpose(k_c * dk_ref[...])
    kd_exp = _tile_rows_masked(kd_t, db, C_WIDTH, ts).astype(BF16)
    sret_ref[...] = s0 * dch_ref[...] + _dot(kd_exp, v_c.astype(BF16))

    oa_ref[...] = (_head_norm(o_a, seg) * gma_ref[...]).astype(BF16)
    oc_ref[...] = (_head_norm(o_c, seg) * gmc_ref[...] * _silu(c_g)).astype(BF16)


def _proj_sample_call(x2, mod_rows, g, w, cst, s0_blk, *, ts, db):
    ns, d = x2.shape
    in_arrays = [
        x2, mod_rows[:, 3 * d:4 * d], mod_rows[:, 4 * d:5 * d], g,
        w["wa"], w["wb"], w["wf"], w["wc"],
        w["gav"], w["gq"], w["gk"], w["bf"], w["gma"], w["gmc"], cst["cos_s"], cst["sin_s"],
        w["wst"], w["bsp_s"], cst["dintra_s"], cst["dq_s"], cst["dk_s"], cst["dch_s"], cst["seg"], s0_blk,
    ]
    out_shape = [
        jax.ShapeDtypeStruct((ns, B_WIDTH), F32), jax.ShapeDtypeStruct((ns, B_WIDTH), F32),
        jax.ShapeDtypeStruct((ns, B_HEADS), F32), jax.ShapeDtypeStruct((ns, B_WIDTH), F32),
        jax.ShapeDtypeStruct((ns, A_WIDTH), F32),
        jax.ShapeDtypeStruct((ns, A_WIDTH), BF16), jax.ShapeDtypeStruct((ns, C_WIDTH), BF16),
        jax.ShapeDtypeStruct(s0_blk.shape, F32),
    ]
    return pl.pallas_call(
        functools.partial(_proj_sample_kernel, ns=ns, ts=ts, db=db),
        out_shape=out_shape,
        compiler_params=pltpu.CompilerParams(vmem_limit_bytes=VMEM_LIMIT),
        name="mix_proj_sample",
    )(*in_arrays)


def _fox_sample_kernel(pt_ref, q_ref, kn_ref, vn_ref, lfn_ref, triu_ref, *rest, ts, gp, page):
    kp_refs = rest[:gp]
    vp_refs = rest[gp:2 * gp]
    lf_refs = rest[2 * gp:3 * gp]
    o_ref = rest[3 * gp]
    qb_scr, qf_scr, m_scr, l_scr, acc_scr, cr_scr = rest[3 * gp + 1:]
    del pt_ref
    pg = pl.program_id(1)
    nrow = ts * B_HEADS
    hmask = (lax.broadcasted_iota(jnp.int32, (B_HEADS, B_WIDTH), 1) // HEAD_DIM
             == lax.broadcasted_iota(jnp.int32, (B_HEADS, B_WIDTH), 0))

    @pl.when(pg == 0)
    def _():
        q = q_ref[...]
        for t in range(ts):
            qt = jnp.where(hmask, jnp.broadcast_to(q[t:t + 1, :], (B_HEADS, B_WIDTH)), 0.0)
            qf_scr[t * B_HEADS:(t + 1) * B_HEADS, :] = qt
            qb_scr[t * B_HEADS:(t + 1) * B_HEADS, :] = qt.astype(BF16)
        m_scr[...] = jnp.full(m_scr.shape, NEG_BIG, F32)
        l_scr[...] = jnp.zeros_like(l_scr)
        acc_scr[...] = jnp.zeros_like(acc_scr)
        cr_scr[...] = jnp.zeros_like(cr_scr)

    qb = qb_scr[...]
    m = m_scr[...]
    l = l_scr[...]
    acc = acc_scr[...]
    carry = cr_scr[:, 0:1]
    lf_all = jnp.concatenate([lf_refs[g][...] for g in range(gp)], axis=0)
    local = _dot_split_lhs(lf_all, triu_ref[...], 3)
    bias = []
    for g in range(gp):
        cum_t = local[g * B_HEADS:(g + 1) * B_HEADS] + carry
        carry = cum_t[:, page - 1:page]
        bias.append(jnp.concatenate([cum_t] * ts, axis=0))
    k_all = jnp.concatenate([kp_refs[g][...].astype(BF16) for g in range(gp)], axis=1)
    s = _dot(qb, k_all) - jnp.concatenate(bias, axis=1)
    m_new = jnp.maximum(m, jnp.max(s, axis=-1, keepdims=True))
    p = jnp.exp(s - m_new)
    alpha = jnp.exp(m - m_new)
    l = alpha * l + jnp.sum(p, axis=-1, keepdims=True)
    v_all = jnp.concatenate([vp_refs[g][...].astype(BF16) for g in range(gp)], axis=1)
    acc = alpha * acc + _dot_nt(p.astype(BF16), v_all)
    m = m_new
    m_scr[...] = m
    l_scr[...] = l
    acc_scr[...] = acc
    cr_scr[...] = jnp.broadcast_to(carry, cr_scr.shape)

    @pl.when(pg == pl.num_programs(1) - 1)
    def _():
        qf = qf_scr[...]
        kn = kn_ref[...]
        vn = vn_ref[...]
        lfn = lfn_ref[...]
        tok = lax.broadcasted_iota(jnp.int32, (nrow, 1), 0) // B_HEADS
        run = carry
        s_new = []
        for jn in range(ts):
            run = run + lfn[:, jn:jn + 1]
            sj = jnp.sum(qf * kn[jn:jn + 1, :], axis=-1, keepdims=True) - jnp.concatenate([run] * ts, axis=0)
            s_new.append(jnp.where(tok >= jn, sj, NEG_BIG))
        m2 = m
        for sj in s_new:
            m2 = jnp.maximum(m2, sj)
        alpha = jnp.exp(m - m2)
        l2 = alpha * l
        acc2 = alpha * acc
        for jn in range(ts):
            pj = jnp.exp(s_new[jn] - m2)
            l2 = l2 + pj
            acc2 = acc2 + pj * vn[jn:jn + 1, :]
        o_full = acc2 / l2
        for t in range(ts):
            blk = jnp.where(hmask, o_full[t * B_HEADS:(t + 1) * B_HEADS, :], 0.0)
            o_ref[t:t + 1, :] = jnp.sum(blk, axis=0, keepdims=True)


def _fox_sample_call(page_table, q3, kn3, vn3, lfn_t, triu, kc, vc, lfc_t, layer, *, gp):
    db, ts, _ = q3.shape
    n_pages = page_table.shape[1]
    page = kc.shape[3]
    nrow = ts * B_HEADS
    seq3 = lambda shape: pl.BlockSpec((None,) + shape, lambda b, pg, pt: (b, 0, 0))

    def page_spec(shape, g):
        return pl.BlockSpec((None, None) + shape, lambda b, pg, pt, g=g: (layer, pt[b, pg * gp + g], 0, 0))

    in_specs = [seq3((ts, B_WIDTH)), seq3((ts, B_WIDTH)), seq3((ts, B_WIDTH)), seq3((B_HEADS, ts)),
                pl.BlockSpec((page, page), lambda b, pg, pt: (0, 0))]
    in_specs += [page_spec((B_WIDTH, page), g) for g in range(gp)]
    in_specs += [page_spec((B_WIDTH, page), g) for g in range(gp)]
    in_specs += [page_spec((B_HEADS, page), g) for g in range(gp)]
    grid_spec = pltpu.PrefetchScalarGridSpec(
        num_scalar_prefetch=1,
        grid=(db, n_pages // gp),
        in_specs=in_specs,
        out_specs=pl.BlockSpec((None, ts, B_WIDTH), lambda b, pg, pt: (b, 0, 0)),
        scratch_shapes=[
            pltpu.VMEM((nrow, B_WIDTH), BF16), pltpu.VMEM((nrow, B_WIDTH), F32),
            pltpu.VMEM((nrow, 1), F32), pltpu.VMEM((nrow, 1), F32),
            pltpu.VMEM((nrow, B_WIDTH), F32), pltpu.VMEM((B_HEADS, LANES), F32),
        ],
    )
    return pl.pallas_call(
        functools.partial(_fox_sample_kernel, ts=ts, gp=gp, page=page),
        grid_spec=grid_spec,
        out_shape=jax.ShapeDtypeStruct((db, ts, B_WIDTH), F32),
        compiler_params=pltpu.CompilerParams(
            dimension_semantics=("arbitrary", "arbitrary"), vmem_limit_bytes=VMEM_LIMIT),
        name="fox_sample",
    )(page_table, q3, kn3, vn3, lfn_t, triu, *([kc] * gp), *([vc] * gp), *([lfc_t] * gp))


def _rope_tables(pos):
    half = HEAD_DIM // 2
    inv = ROPE_BASE ** (-jnp.arange(half, dtype=F32) / half)
    ang = pos[:, None] * inv[None, :]
    cos = jnp.cos(ang)
    sin = jnp.sin(ang)
    cos_h = jnp.concatenate([cos, cos], axis=-1)
    sin_h = jnp.concatenate([-sin, sin], axis=-1)
    return jnp.tile(cos_h, (1, C_HEADS)), jnp.tile(sin_h, (1, C_HEADS))


def _decay_tables(length, log_gamma):
    idx = jnp.arange(length, dtype=F32)
    rel = idx[:, None] - idx[None, :]
    d_intra = jnp.where(rel[None] >= 0, jnp.exp(jnp.maximum(rel, 0.0)[None] * log_gamma[:, None, None]), 0.0)
    d_q = jnp.exp((idx + 1.0)[:, None] * log_gamma[None, :])
    d_k = jnp.exp((length - 1.0 - idx)[:, None] * log_gamma[None, :])
    d_chunk = jnp.exp(length * log_gamma)
    return d_intra, d_q, d_k, d_chunk


def _constants(t, ts, db, past, tm):
    log_gamma = jnp.log1p(-jnp.exp2(-5.0 - jnp.arange(C_HEADS, dtype=F32)))
    cst = {}
    cst["cos_p"], cst["sin_p"] = _rope_tables(jnp.arange(t, dtype=F32))
    cos_s, sin_s = _rope_tables(past + jnp.arange(ts, dtype=F32))
    cst["cos_s"], cst["sin_s"] = jnp.tile(cos_s, (db, 1)), jnp.tile(sin_s, (db, 1))
    rep = lambda a: jnp.repeat(a, HEAD_DIM, axis=-1)

    d_intra, d_q, d_k, d_chunk = _decay_tables(CHUNK, log_gamma)
    cst["dintra_p"] = jnp.concatenate([d_intra[h] for h in range(C_HEADS)], axis=1)
    cst["dq_p"], cst["dk_p"], cst["dch_p"] = rep(d_q), rep(d_k), rep(d_chunk[None, :])

    d_intra, d_q, d_k, d_chunk = _decay_tables(ts, log_gamma)
    same_seq = jnp.kron(jnp.eye(db, dtype=F32), jnp.ones((ts, ts), F32))
    cst["dintra_s"] = jnp.concatenate(
        [jnp.tile(d_intra[h], (db, db)) * same_seq for h in range(C_HEADS)], axis=1)
    cst["dq_s"] = jnp.tile(rep(d_q), (db, 1))
    cst["dk_s"] = jnp.tile(rep(d_k), (db, 1))
    cst["dch_s"] = rep(d_chunk[None, :])

    seg = jnp.kron(jnp.eye(MXU_DIM // HEAD_DIM, dtype=F32), jnp.ones((HEAD_DIM, HEAD_DIM), F32))
    cst["seg"] = seg.astype(BF16)
    cst["tril"] = jnp.tril(jnp.ones((tm, tm), F32)).astype(BF16)

    npiece = 3
    pk = np.zeros((npiece * LANES, (B_HEADS // 2) * LANES), np.float32)
    kconst = np.zeros((1, (B_HEADS // 2) * LANES), np.float32)
    pq = np.zeros((npiece * LANES, B_HEADS * LANES), np.float32)
    qconst = np.zeros((1, B_HEADS * LANES), np.float32)
    for h in range(B_HEADS):
        hp, i = divmod(h, 2)
        for a in range(npiece):
            kconst[0, hp * LANES + a] = 1.0
            pq[a * LANES + h, h * LANES + a] = 1.0
            pk[a * LANES + h, hp * LANES + npiece * (1 + i) + a] = 1.0
            qconst[0, h * LANES + npiece * (1 + i) + a] = -1.0
    cst["pk"], cst["pq"] = jnp.asarray(pk, BF16), jnp.asarray(pq, BF16)
    cst["kconst"], cst["qconst"] = jnp.asarray(kconst), jnp.asarray(qconst)
    return cst


def _layer_weights(l, w_in, g_a_v, w_spatial, b_spatial, g_qnorm, g_knorm, b_forget, g_mix, ts, db):
    d = w_in.shape[1]
    wl = w_in[l]
    o = 0
    wa = wl[:, o:o + 2 * A_WIDTH]; o += 2 * A_WIDTH
    wb = wl[:, o:o + 3 * B_WIDTH]; o += 3 * B_WIDTH
    wf = wl[:, o:o + B_HEADS]; o += B_HEADS
    wc = wl[:, o:o + 4 * C_WIDTH]
    w = {
        "wa": wa.astype(BF16), "wb": wb.astype(BF16), "wc": wc.astype(BF16),
        "wf": jnp.pad(wf, ((0, 0), (0, LANES - B_HEADS))).astype(BF16),
        "gav": g_a_v[l][None, :],
        "gq": jnp.tile(g_qnorm[l], B_HEADS)[None, :] * (HEAD_DIM ** -0.5),
        "gk": jnp.tile(g_knorm[l], B_HEADS)[None, :],
        "bf": jnp.pad(b_forget[l], (0, LANES - B_HEADS))[None, :],
        "gma": g_mix[l][None, :A_WIDTH],
        "gmb": g_mix[l][None, A_WIDTH:A_WIDTH + B_WIDTH],
        "gmc": g_mix[l][None, A_WIDTH + B_WIDTH:],
        "ws": w_spatial[l],
        "bsp_p": jnp.repeat(b_spatial[l].T, HEAD_DIM, axis=-1),
        "wst": jnp.tile(w_spatial[l][:, :ts, :ts], (1, db, db)),
        "bsp_s": jnp.tile(jnp.repeat(b_spatial[l][:, :ts].T, HEAD_DIM, axis=-1), (db, 1)),
    }
    del d
    return w


def kernel(x_prompt, x_sample, cache_k, cache_v, cache_logf, state_ret, page_table, c_prompt, c_sample,
           g_norm, w_ada, b_ada, w_ffn_gate, w_ffn_up, w_ffn_down, w_in, g_a_v, w_spatial, b_spatial,
           g_qnorm, g_knorm, b_forget, g_mix, w_out):
    b, t, d = x_prompt.shape
    db, ts, _ = x_sample.shape
    depth = w_in.shape[0]
    ns = db * ts
    n_pool, page = cache_k.shape[1], cache_k.shape[2]
    n_pages = page_table.shape[1]
    past = n_pages * page
    d_ff = w_ffn_gate.shape[-1]

    tm_ffn = min(512, t)
    tm_mix = min(512, t)
    ta = min(512, t)
    nf = 2
    tf = d_ff // nf
    gp = min(16, n_pages)

    cst = _constants(t, ts, db, past, tm_mix)
    triu_page = jnp.triu(jnp.ones((page, page), F32)).astype(BF16)

    mod = _ada_call(jnp.concatenate([c_prompt, c_sample], axis=0), w_ada, b_ada)

    kc = jnp.transpose(cache_k, (0, 1, 3, 4, 2)).reshape(depth, n_pool, B_WIDTH, page)
    vc = jnp.transpose(cache_v, (0, 1, 3, 4, 2)).reshape(depth, n_pool, B_WIDTH, page)
    lfc_t = jnp.swapaxes(cache_logf, 2, 3)

    x_p = x_prompt.reshape(b * t, d)
    x_s = x_sample.reshape(ns, d)
    outs = {k: [] for k in ("kp", "vp", "lp", "rp", "ks", "vs", "ls", "rs", "av")}
    head_idx = jnp.arange(C_HEADS)
    for l in range(depth):
        mod_p = mod[l, :b].reshape(b, 1, -1)
        mod_s = jnp.repeat(mod[l, b:], ts, axis=0)
        w = _layer_weights(l, w_in, g_a_v, w_spatial, b_spatial, g_qnorm, g_knorm, b_forget, g_mix, ts, db)
        ffn_w = []
        for i in range(2):
            wg3 = w_ffn_gate[l, i].astype(BF16).reshape(d, nf, tf).transpose(1, 0, 2)
            wu3 = w_ffn_up[l, i].astype(BF16).reshape(d, nf, tf).transpose(1, 0, 2)
            wd3 = w_ffn_down[l, i].astype(BF16).reshape(nf, tf, d)
            ffn_w.append((wg3, wu3, wd3))
        w_out_l = w_out[l].astype(BF16)

        x_p = _ffn_call(x_p, mod_p, 0, g_norm[l, 0][None, :], *ffn_w[0], per_row=False, tm=tm_ffn, seq_len=t)
        x_s = _ffn_call(x_s, mod_s, 0, g_norm[l, 0][None, :], *ffn_w[0], per_row=True, tm=ns, seq_len=ts)

        (kt32, vt32, logf, qaug, kaug, vt16, oa, oc, sret) = _proj_prompt_call(
            x_p.reshape(b, t, d), mod_p, g_norm[l, 1][None, :], w, cst, tm=tm_mix, ta=ta)
        ob = _fox_prompt_call(qaug, kaug, vt16, ta=ta)
        x_p = _out_call(oa.reshape(b * t, -1), ob.reshape(b * t, -1), oc.reshape(b * t, -1), x_p, mod_p,
                        w["gmb"], cst["seg"], w_out_l, per_row=False, tm=tm_mix, seq_len=t)
        outs["kp"].append(jnp.transpose(kt32.reshape(b, B_HEADS, HEAD_DIM, t), (0, 3, 1, 2)))
        outs["vp"].append(jnp.transpose(vt32.reshape(b, B_HEADS, HEAD_DIM, t), (0, 3, 1, 2)))
        outs["lp"].append(logf)
        s5 = sret.reshape(b, C_HEADS, HEAD_DIM, C_HEADS, HEAD_DIM)
        outs["rp"].append(jnp.moveaxis(s5[:, head_idx, :, head_idx, :], 0, 1))

        s0_blk = jnp.einsum("bhde,hg->bhdge", state_ret[l], jnp.eye(C_HEADS, dtype=F32)).reshape(
            db * C_WIDTH, C_WIDTH)
        (k32s, v32s, logfs, q32s, avs, oas, ocs, srets) = _proj_sample_call(
            x_s, mod_s, g_norm[l, 1][None, :], w, cst, s0_blk, ts=ts, db=db)
        obs = _fox_sample_call(
            page_table, q32s.reshape(db, ts, -1), k32s.reshape(db, ts, -1), v32s.reshape(db, ts, -1),
            jnp.swapaxes(logfs.reshape(db, ts, B_HEADS), 1, 2), triu_page, kc, vc, lfc_t, l, gp=gp)
        x_s = _out_call(oas, obs.reshape(ns, -1), ocs, x_s, mod_s,
                        w["gmb"], cst["seg"], w_out_l, per_row=True, tm=ns, seq_len=ts)
        outs["ks"].append(k32s.reshape(db, ts, B_HEADS, HEAD_DIM))
        outs["vs"].append(v32s.reshape(db, ts, B_HEADS, HEAD_DIM))
        outs["ls"].append(logfs.reshape(db, ts, B_HEADS))
        s5 = srets.reshape(db, C_HEADS, HEAD_DIM, C_HEADS, HEAD_DIM)
        outs["rs"].append(jnp.moveaxis(s5[:, head_idx, :, head_idx, :], 0, 1))
        outs["av"].append(avs.reshape(db, ts, A_WIDTH))

        x_p = _ffn_call(x_p, mod_p, 2, g_norm[l, 2][None, :], *ffn_w[1], per_row=False, tm=tm_ffn, seq_len=t)
        x_s = _ffn_call(x_s, mod_s, 2, g_norm[l, 2][None, :], *ffn_w[1], per_row=True, tm=ns, seq_len=ts)

    st = lambda key: jnp.stack(outs[key])
    return (x_p.reshape(b, t, d), x_s.reshape(db, ts, d), st("kp"), st("vp"), st("lp"), st("rp"),
            st("ks"), st("vs"), st("ls"), st("rs"), st("av"))
```

```python
import functools

import numpy as np
import jax
import jax.numpy as jnp
from jax import lax
from jax.experimental import pallas as pl
from jax.experimental.pallas import tpu as pltpu

F32 = jnp.float32
BF16 = jnp.bfloat16

HEAD_DIM = 64
A_HEADS, B_HEADS, C_HEADS = 4, 8, 4
A_WIDTH, B_WIDTH, C_WIDTH = A_HEADS * HEAD_DIM, B_HEADS * HEAD_DIM, C_HEADS * HEAD_DIM
CHUNK = 128
ROPE_BASE = 10000.0
EPS = 1e-6
N_SUBLAYERS = 3
LANES = 128
MXU_DIM = 256
NEG_BIG = -1e30
FOX_HEADS_PER_STEP = 4
V_ROWS = 80
LOG2E = 1.4426950408889634
VMEM_LIMIT = 56 * 1024 * 1024


def _dot(a, b):
    return jnp.dot(a, b, preferred_element_type=F32)


def _dot_nt(a, b):
    return lax.dot_general(a, b, (((1,), (1,)), ((), ())), preferred_element_type=F32)


def _silu(x):
    return x / (1.0 + jnp.exp(-x))


def _gelu_tanh(x):
    c = 0.7978845608028654
    return 0.5 * x * (1.0 + jnp.tanh(c * (x + 0.044715 * (x * x * x))))


def _log_sigmoid(x):
    return jnp.minimum(x, 0.0) - jnp.log1p(jnp.exp(-jnp.abs(x)))


def _modnorm(x, g, shift, scale):
    ms = jnp.mean(x * x, axis=-1, keepdims=True)
    return (x * lax.rsqrt(ms + EPS) * g) * (1.0 + scale) + shift


def _split_bf16(x, parts):
    out = []
    r = x
    for i in range(parts):
        p = r.astype(BF16)
        out.append(p)
        if i + 1 < parts:
            r = r - p.astype(F32)
    return out


def _dot_split_lhs(a, b_bf16, parts):
    acc = None
    for p in _split_bf16(a, parts):
        t = _dot(p, b_bf16)
        acc = t if acc is None else acc + t
    return acc


def _dot_split_rhs(a_bf16, b, parts):
    acc = None
    for p in _split_bf16(b, parts):
        t = _dot(a_bf16, p)
        acc = t if acc is None else acc + t
    return acc


def _head_mean_sq(x, seg):
    w = x.shape[-1]
    outs = []
    for c in range(w // MXU_DIM):
        xs = x[:, c * MXU_DIM:(c + 1) * MXU_DIM]
        outs.append(_dot_split_lhs(xs * xs, seg, 2))
    ss = outs[0] if len(outs) == 1 else jnp.concatenate(outs, axis=-1)
    return ss * (1.0 / HEAD_DIM)


def _head_norm(x, seg):
    return x * lax.rsqrt(_head_mean_sq(x, seg) + EPS)


def _rope(x, cos, sin_signed):
    outs = []
    for c in range(x.shape[-1] // LANES):
        xs = x[:, c * LANES:(c + 1) * LANES]
        fwd = pltpu.roll(xs, HEAD_DIM // 2, axis=1)
        bwd = pltpu.roll(xs, LANES - HEAD_DIM // 2, axis=1)
        lane = lax.broadcasted_iota(jnp.int32, xs.shape, 1)
        outs.append(jnp.where((lane % HEAD_DIM) < HEAD_DIM // 2, bwd, fwd))
    partner = jnp.concatenate(outs, axis=-1)
    return x * cos + partner * sin_signed


def _tile_rows_masked(x, nrep, rows_per, cols_per):
    xt = jnp.concatenate([x] * nrep, axis=0)
    r = lax.broadcasted_iota(jnp.int32, xt.shape, 0) // rows_per
    c = lax.broadcasted_iota(jnp.int32, xt.shape, 1) // cols_per
    return jnp.where(r == c, xt, 0.0)


def _tile_cols_masked(x, nrep, rows_per, cols_per):
    xt = jnp.concatenate([x] * nrep, axis=1)
    r = lax.broadcasted_iota(jnp.int32, xt.shape, 0) // rows_per
    c = lax.broadcasted_iota(jnp.int32, xt.shape, 1) // cols_per
    return jnp.where(r == c, xt, 0.0)


def _ada_kernel(c_ref, w_ref, b_ref, o_ref):
    a = _silu(c_ref[...]).astype(BF16)
    o_ref[...] = _dot(a, w_ref[...].astype(BF16)) + b_ref[...]


def _ada_call(c_all, w_ada, b_ada):
    depth, d, ncol = w_ada.shape
    rows = c_all.shape[0]
    tn = 1024
    return pl.pallas_call(
        _ada_kernel,
        grid=(depth, ncol // tn),
        in_specs=[
            pl.BlockSpec((rows, d), lambda l, n: (0, 0)),
            pl.BlockSpec((None, d, tn), lambda l, n: (l, 0, n)),
            pl.BlockSpec((None, 1, tn), lambda l, n: (l, 0, n)),
        ],
        out_specs=pl.BlockSpec((None, rows, tn), lambda l, n: (l, 0, n)),
        out_shape=jax.ShapeDtypeStruct((depth, rows, ncol), F32),
        compiler_params=pltpu.CompilerParams(
            dimension_semantics=("arbitrary", "arbitrary"), vmem_limit_bytes=VMEM_LIMIT),
        name="adaln",
    )(c_all, w_ada, b_ada.reshape(depth, 1, ncol))


def _mod_specs(per_row, tm, d, tiles_per_seq, sub, which):
    col = sub * 3 + which
    if per_row:
        return pl.BlockSpec((tm, d), lambda i: (i, col))
    return pl.BlockSpec((None, 1, d), lambda i: (i // tiles_per_seq, 0, col))


def _half_ffn(x, shift, scale, gate_mod, g, wg_ref, wu_ref, wd_ref, nf):
    d_ff = wg_ref.shape[1]
    tf = d_ff // nf
    h = _modnorm(x, g, shift, scale).astype(BF16)
    acc = None
    for f in range(nf):
        cols = slice(f * tf, (f + 1) * tf)
        gate = _dot(h, wg_ref[:, cols])
        up = _dot(h, wu_ref[:, cols])
        a = (_silu(gate) * up).astype(BF16)
        t = _dot(a, wd_ref[cols, :])
        acc = t if acc is None else acc + t
    return x + (0.5 * gate_mod) * acc


def _ffn_kernel(x_ref, sh_ref, sc_ref, gt_ref, g_ref, wg_ref, wu_ref, wd_ref, o_ref, *, nf):
    o_ref[...] = _half_ffn(x_ref[...], sh_ref[...], sc_ref[...], gt_ref[...], g_ref[...],
                           wg_ref, wu_ref, wd_ref, nf)


def _merge_ffn_kernel(oa_ref, ob_ref, oc_ref, x_ref, gt1_ref, gmb_ref, seg_ref, wo_ref,
                      sh_ref, sc_ref, gt_ref, g_ref, wg_ref, wu_ref, wd_ref, o_ref, *, nf):
    ob = (_head_norm(ob_ref[...], seg_ref[...]) * gmb_ref[...]).astype(BF16)
    ab = A_WIDTH + B_WIDTH
    y = (_dot(oa_ref[...], wo_ref[:A_WIDTH, :]) + _dot(ob, wo_ref[A_WIDTH:ab, :])
         + _dot(oc_ref[...], wo_ref[ab:, :]))
    x = x_ref[...] + gt1_ref[...] * y
    o_ref[...] = _half_ffn(x, sh_ref[...], sc_ref[...], gt_ref[...], g_ref[...], wg_ref, wu_ref, wd_ref, nf)


def _ffn_call(x2, mod, sub, g, wg, wu, wd, *, per_row, tm, seq_len, nf, merge=None):
    n, d = x2.shape
    d_ff = wg.shape[1]
    tiles_per_seq = seq_len // tm if not per_row else 1
    const2 = lambda i: (0, 0)
    rows = lambda width: pl.BlockSpec((tm, width), lambda i: (i, 0))
    resident = lambda shape: pl.BlockSpec(shape, const2, pipeline_mode=pl.Buffered(1))
    ffn_specs = [
        _mod_specs(per_row, tm, d, tiles_per_seq, sub, 0),
        _mod_specs(per_row, tm, d, tiles_per_seq, sub, 1),
        _mod_specs(per_row, tm, d, tiles_per_seq, sub, 2),
        pl.BlockSpec((1, d), const2),
        resident((d, d_ff)), resident((d, d_ff)), resident((d_ff, d)),
    ]
    ffn_args = (mod, mod, mod, g, wg, wu, wd)
    if merge is None:
        body, in_specs, args = _ffn_kernel, [rows(d)] + ffn_specs, (x2,) + ffn_args
    else:
        oa, ob, oc, gmb, seg, w_out = merge
        body = _merge_ffn_kernel
        in_specs = [rows(A_WIDTH), rows(B_WIDTH), rows(C_WIDTH), rows(d),
                    _mod_specs(per_row, tm, d, tiles_per_seq, 1, 2),
                    pl.BlockSpec((1, B_WIDTH), const2), pl.BlockSpec((MXU_DIM, MXU_DIM), const2),
                    resident((d, d))] + ffn_specs
        args = (oa, ob, oc, x2, mod, gmb, seg, w_out) + ffn_args
    return pl.pallas_call(
        functools.partial(body, nf=nf),
        grid=(n // tm,),
        in_specs=in_specs,
        out_specs=rows(d),
        out_shape=jax.ShapeDtypeStruct((n, d), F32),
        compiler_params=pltpu.CompilerParams(
            dimension_semantics=("arbitrary",), vmem_limit_bytes=VMEM_LIMIT),
        name="half_ffn" if merge is None else "merge_ffn",
    )(*args)


def _proj_prompt_kernel(
        x_ref, sh_ref, sc_ref, g_ref, wa_ref, wb_ref, wf_ref, wc_ref,
        gav_ref, gq_ref, gk_ref, bf_ref, gma_ref, gmc_ref, cos_ref, sin_ref,
        ws_ref, bsp_ref, dintra_ref, dq_ref, dk_ref, dch_ref, seg_ref, tril_ref,
        pk_ref, kconst_ref, pq_ref, qconst_ref,
        kt32_ref, vt32_ref, logf_ref, qaug_ref, kaug_ref, vt16_ref,
        oa_ref, oc_ref, sret_ref,
        s_scr, cc_scr, *, tm, ta):
    j = pl.program_id(1)

    @pl.when(j == 0)
    def _():
        s_scr[...] = jnp.zeros_like(s_scr)
        cc_scr[...] = jnp.zeros_like(cc_scr)

    x = x_ref[...]
    hb = _modnorm(x, g_ref[...], sh_ref[...], sc_ref[...]).astype(BF16)
    seg = seg_ref[...]

    za = _dot(hb, wa_ref[...])
    zb = _dot(hb, wb_ref[...])
    zf = _dot(hb, wf_ref[...])
    zc = _dot(hb, wc_ref[...])

    a_u = _gelu_tanh(za[:, :A_WIDTH])
    a_vr = _gelu_tanh(za[:, A_WIDTH:])
    a_v = a_vr * lax.rsqrt(jnp.mean(a_vr * a_vr, axis=-1, keepdims=True) + EPS) * gav_ref[...]

    q_b = _head_norm(zb[:, :B_WIDTH], seg) * gq_ref[...]
    k_b = _head_norm(zb[:, B_WIDTH:2 * B_WIDTH], seg) * gk_ref[...]
    v_b = zb[:, 2 * B_WIDTH:]
    kt32_ref[...] = jnp.transpose(k_b)
    v_t = jnp.transpose(v_b)
    vt32_ref[...] = v_t
    v_t16 = v_t.astype(BF16)
    pad_row = lax.broadcasted_iota(jnp.int32, (V_ROWS - HEAD_DIM, ta), 0)
    ones_pad = jnp.where(pad_row == 0, 1.0, 0.0).astype(BF16)
    for h in range(B_HEADS):
        for c in range(tm // ta):
            vt16_ref[h, c] = jnp.concatenate(
                [v_t16[h * HEAD_DIM:(h + 1) * HEAD_DIM, c * ta:(c + 1) * ta], ones_pad], axis=0)

    logf = _log_sigmoid(zf + bf_ref[...])
    logf_ref[...] = logf[:, :B_HEADS]
    cum_c = _dot_split_rhs(tril_ref[...], logf, 3) + cc_scr[...]
    cc_scr[...] = cum_c[tm - 1:tm, :]
    pieces = jnp.concatenate(_split_bf16(cum_c * LOG2E, 3), axis=-1)
    k_aux = (_dot(pieces, pk_ref[...]) + kconst_ref[...]).astype(BF16)
    q_aux = (_dot(pieces, pq_ref[...]) + qconst_ref[...]).astype(BF16)
    q16 = (q_b * LOG2E).astype(BF16)
    k16 = k_b.astype(BF16)
    lane = lax.broadcasted_iota(jnp.int32, (tm, LANES), 1)
    for hp in range(B_HEADS // 2):
        pair = slice(hp * LANES, (hp + 1) * LANES)
        kaug_ref[hp, :, :LANES] = k16[:, pair]
        kaug_ref[hp, :, LANES:] = k_aux[:, pair]
        for i in range(2):
            h = 2 * hp + i
            own = (lane < HEAD_DIM) if i == 0 else (lane >= HEAD_DIM)
            qaug_ref[h, :, :LANES] = jnp.where(own, q16[:, pair], jnp.zeros_like(q16[:, pair]))
            qaug_ref[h, :, LANES:] = q_aux[:, h * LANES:(h + 1) * LANES]

    cos = cos_ref[...]
    sin = sin_ref[...]
    q_c = _rope(zc[:, :C_WIDTH], cos, sin)
    k_c = _rope(zc[:, C_WIDTH:2 * C_WIDTH], cos, sin) * (HEAD_DIM ** -0.5)
    v_c = zc[:, 2 * C_WIDTH:3 * C_WIDTH]
    c_g = zc[:, 3 * C_WIDTH:]

    row = lax.broadcasted_iota(jnp.int32, (CHUNK, CHUNK), 0)
    col = lax.broadcasted_iota(jnp.int32, (CHUNK, CHUNK), 1)
    w_cat = jnp.concatenate(
        [jnp.where(col <= row, ws_ref[h], 0.0) for h in range(A_HEADS)], axis=1).astype(BF16)
    dintra = dintra_ref[...]
    dq = dq_ref[...]
    dk = dk_ref[...]
    dch = dch_ref[...]
    bsp = bsp_ref[...]
    sr = lax.broadcasted_iota(jnp.int32, (C_WIDTH, C_WIDTH), 0) // HEAD_DIM
    scol = lax.broadcasted_iota(jnp.int32, (C_WIDTH, C_WIDTH), 1) // HEAD_DIM
    s_mask = sr == scol
    s_blk = s_scr[...]
    oa_parts = []
    oc_parts = []
    for c in range(tm // CHUNK):
        sl = slice(c * CHUNK, (c + 1) * CHUNK)
        av_blk = _tile_rows_masked(a_v[sl], A_HEADS, CHUNK, HEAD_DIM).astype(BF16)
        oa_parts.append(a_u[sl] * (_dot(w_cat, av_blk) + bsp))

        qc = q_c[sl].astype(BF16)
        k_blk = _tile_rows_masked(k_c[sl], C_HEADS, CHUNK, HEAD_DIM).astype(BF16)
        v_blk = _tile_rows_masked(v_c[sl], C_HEADS, CHUNK, HEAD_DIM).astype(BF16)
        att = _dot_nt(qc, k_blk) * dintra
        intra = _dot(att.astype(BF16), v_blk)
        inter = _dot(qc, s_blk.astype(BF16)) * dq
        oc_parts.append(intra + inter)
        kd_t = jnp.transpose(k_c[sl] * dk).astype(BF16)
        upd = _dot(kd_t, v_c[sl].astype(BF16))
        s_blk = s_blk * dch + jnp.where(s_mask, upd, 0.0)
    s_scr[...] = s_blk
    sret_ref[...] = s_blk

    o_a = oa_parts[0] if len(oa_parts) == 1 else jnp.concatenate(oa_parts, axis=0)
    o_c = oc_parts[0] if len(oc_parts) == 1 else jnp.concatenate(oc_parts, axis=0)
    oa_ref[...] = (_head_norm(o_a, seg) * gma_ref[...]).astype(BF16)
    oc_ref[...] = (_head_norm(o_c, seg) * gmc_ref[...] * _silu(c_g)).astype(BF16)


def _proj_prompt_call(x3, mod, g, w, cst, *, tm, ta):
    b, t, d = x3.shape
    nt = t // tm
    full2 = lambda bb, j: (0, 0)
    full3 = lambda bb, j: (0, 0, 0)
    row_spec = lambda width: pl.BlockSpec((None, tm, width), lambda bb, j: (bb, j, 0))
    mod_spec = lambda col: pl.BlockSpec((None, 1, d), lambda bb, j: (bb, 0, col))
    vec = lambda width: pl.BlockSpec((1, width), full2)
    in_specs = [
        row_spec(d), mod_spec(3), mod_spec(4), vec(d),
        pl.BlockSpec((d, 2 * A_WIDTH), full2), pl.BlockSpec((d, 3 * B_WIDTH), full2),
        pl.BlockSpec((d, LANES), full2), pl.BlockSpec((d, 4 * C_WIDTH), full2),
        vec(A_WIDTH), vec(B_WIDTH), vec(B_WIDTH), vec(LANES), vec(A_WIDTH), vec(C_WIDTH),
        pl.BlockSpec((tm, C_WIDTH), lambda bb, j: (j, 0)),
        pl.BlockSpec((tm, C_WIDTH), lambda bb, j: (j, 0)),
        pl.BlockSpec((A_HEADS, CHUNK, CHUNK), full3),
        pl.BlockSpec((CHUNK, A_WIDTH), full2),
        pl.BlockSpec((CHUNK, C_HEADS * CHUNK), full2),
        pl.BlockSpec((CHUNK, C_WIDTH), full2), pl.BlockSpec((CHUNK, C_WIDTH), full2),
        vec(C_WIDTH),
        pl.BlockSpec((MXU_DIM, MXU_DIM), full2),
        pl.BlockSpec((tm, tm), full2),
        pl.BlockSpec((3 * LANES, (B_HEADS // 2) * LANES), full2), vec((B_HEADS // 2) * LANES),
        pl.BlockSpec((3 * LANES, B_HEADS * LANES), full2), vec(B_HEADS * LANES),
    ]
    nhp = B_HEADS // 2
    col_spec = pl.BlockSpec((None, B_WIDTH, tm), lambda bb, j: (bb, 0, j))
    out_shape = [
        jax.ShapeDtypeStruct((b, B_WIDTH, t), F32), jax.ShapeDtypeStruct((b, B_WIDTH, t), F32),
        jax.ShapeDtypeStruct((b, t, B_HEADS), F32),
        jax.ShapeDtypeStruct((b, B_HEADS, t, 2 * LANES), BF16),
        jax.ShapeDtypeStruct((b, nhp, t, 2 * LANES), BF16),
        jax.ShapeDtypeStruct((b, B_HEADS, t // ta, V_ROWS, ta), BF16),
        jax.ShapeDtypeStruct((b, t, A_WIDTH), BF16), jax.ShapeDtypeStruct((b, t, C_WIDTH), BF16),
        jax.ShapeDtypeStruct((b, C_WIDTH, C_WIDTH), F32),
    ]
    out_specs = [
        col_spec, col_spec, row_spec(B_HEADS),
        pl.BlockSpec((None, B_HEADS, tm, 2 * LANES), lambda bb, j: (bb, 0, j, 0)),
        pl.BlockSpec((None, nhp, tm, 2 * LANES), lambda bb, j: (bb, 0, j, 0)),
        pl.BlockSpec((None, B_HEADS, tm // ta, V_ROWS, ta), lambda bb, j: (bb, 0, j, 0, 0)),
        row_spec(A_WIDTH), row_spec(C_WIDTH),
        pl.BlockSpec((None, C_WIDTH, C_WIDTH), lambda bb, j: (bb, 0, 0)),
    ]
    return pl.pallas_call(
        functools.partial(_proj_prompt_kernel, tm=tm, ta=ta),
        grid=(b, nt),
        in_specs=in_specs,
        out_specs=out_specs,
        out_shape=out_shape,
        scratch_shapes=[
            pltpu.VMEM((C_WIDTH, C_WIDTH), F32),
            pltpu.VMEM((1, LANES), F32),
        ],
        compiler_params=pltpu.CompilerParams(
            dimension_semantics=("arbitrary", "arbitrary"), vmem_limit_bytes=VMEM_LIMIT),
        name="mix_proj_prompt",
    )(x3, mod, mod, g, w["wa"], w["wb"], w["wf"], w["wc"],
      w["gav"], w["gq"], w["gk"], w["bf"], w["gma"], w["gmc"], cst["cos_p"], cst["sin_p"],
      w["ws"], w["bsp_p"], cst["dintra_p"], cst["dq_p"], cst["dk_p"], cst["dch_p"],
      cst["seg"], cst["tril"], cst["pk"], cst["kconst"], cst["pq"], cst["qconst"])


def _fox_prompt_kernel(q_ref, k_ref, vt_ref, o_ref, *, ta, nh):
    j = pl.program_id(2)
    q_heads = [q_ref[h] for h in range(nh)]
    kpos = lax.broadcasted_iota(jnp.int32, (ta, ta), 0)
    qpos = lax.broadcasted_iota(jnp.int32, (ta, ta), 1)
    causal = kpos <= qpos

    def step(kb, carry, masked):
        start = pl.multiple_of(kb * ta, ta)
        kblks = [k_ref[hp, pl.ds(start, ta), :] for hp in range(nh // 2)]
        scores = [_dot_nt(kblks[h // 2], q_heads[h]) for h in range(nh)]
        new = []
        for h, (m, acc) in enumerate(carry):
            s = scores[h]
            if masked:
                s = jnp.where(causal, s, NEG_BIG)
            m_new = jnp.maximum(m, jnp.max(s, axis=0, keepdims=True))
            p = jnp.exp2(s - m_new)
            alpha = jnp.exp2(m - m_new)
            acc = alpha * acc + _dot(vt_ref[h, kb], p.astype(BF16))
            new.append((m_new, acc))
        return tuple(new)

    init = tuple((jnp.full((1, ta), NEG_BIG, F32), jnp.zeros((V_ROWS, ta), F32)) for _ in range(nh))
    carry = lax.fori_loop(0, j, functools.partial(step, masked=False), init)
    carry = step(j, carry, True)
    o_t = jnp.concatenate([acc[:HEAD_DIM] / acc[HEAD_DIM:HEAD_DIM + 1] for (_, acc) in carry], axis=0)
    o_ref[...] = jnp.transpose(o_t)


def _fox_prompt_call(qaug, kaug, vt16, *, ta, nh):
    b, _, t, wq = qaug.shape
    return pl.pallas_call(
        functools.partial(_fox_prompt_kernel, ta=ta, nh=nh),
        grid=(b, B_HEADS // nh, t // ta),
        in_specs=[
            pl.BlockSpec((None, nh, ta, wq), lambda bb, hg, j: (bb, hg, j, 0)),
            pl.BlockSpec((None, nh // 2, t, wq), lambda bb, hg, j: (bb, hg, 0, 0)),
            pl.BlockSpec((None, nh, t // ta, V_ROWS, ta), lambda bb, hg, j: (bb, hg, 0, 0, 0)),
        ],
        out_specs=pl.BlockSpec((None, ta, nh * HEAD_DIM), lambda bb, hg, j: (bb, j, hg)),
        out_shape=jax.ShapeDtypeStruct((b, t, B_WIDTH), F32),
        compiler_params=pltpu.CompilerParams(
            dimension_semantics=("arbitrary", "arbitrary", "arbitrary"), vmem_limit_bytes=VMEM_LIMIT),
        name="fox_prompt",
    )(qaug, kaug, vt16)


def _proj_sample_kernel(
        x_ref, sh_ref, sc_ref, g_ref, wa_ref, wb_ref, wf_ref, wc_ref,
        gav_ref, gq_ref, gk_ref, bf_ref, gma_ref, gmc_ref, cos_ref, sin_ref,
        wst_ref, bsp_ref, dintra_ref, dq_ref, dk_ref, dch_ref, seg_ref, s0_ref,
        k32_ref, v32_ref, logf_ref, q32_ref, av_ref, oa_ref, oc_ref, sret_ref, *, ns, ts, db):
    x = x_ref[...]
    hb = _modnorm(x, g_ref[...], sh_ref[...], sc_ref[...]).astype(BF16)
    seg = seg_ref[...]

    za = _dot(hb, wa_ref[...])
    a_u = _gelu_tanh(za[:, :A_WIDTH])
    a_vr = _gelu_tanh(za[:, A_WIDTH:])
    a_v = a_vr * lax.rsqrt(jnp.mean(a_vr * a_vr, axis=-1, keepdims=True) + EPS) * gav_ref[...]
    av_ref[...] = a_v

    zb = _dot(hb, wb_ref[...])
    q32_ref[...] = _head_norm(zb[:, :B_WIDTH], seg) * gq_ref[...]
    k32_ref[...] = _head_norm(zb[:, B_WIDTH:2 * B_WIDTH], seg) * gk_ref[...]
    v32_ref[...] = zb[:, 2 * B_WIDTH:]
    logf = _log_sigmoid(_dot(hb, wf_ref[...]) + bf_ref[...])
    logf_ref[...] = logf[:, :B_HEADS]

    zc = _dot(hb, wc_ref[...])
    cos = cos_ref[...]
    sin = sin_ref[...]
    q_c = _rope(zc[:, :C_WIDTH], cos, sin)
    k_c = _rope(zc[:, C_WIDTH:2 * C_WIDTH], cos, sin) * (HEAD_DIM ** -0.5)
    v_c = zc[:, 2 * C_WIDTH:3 * C_WIDTH]
    c_g = zc[:, 3 * C_WIDTH:]

    row = lax.broadcasted_iota(jnp.int32, (ns, ns), 0)
    col = lax.broadcasted_iota(jnp.int32, (ns, ns), 1)
    keep = (row // ts == col // ts) & (col <= row)
    w_cat = jnp.concatenate(
        [jnp.where(keep, wst_ref[h], 0.0) for h in range(A_HEADS)], axis=1).astype(BF16)
    av_blk = _tile_rows_masked(a_v, A_HEADS, ns, HEAD_DIM).astype(BF16)
    o_a = a_u * (_dot(w_cat, av_blk) + bsp_ref[...])

    qc = q_c.astype(BF16)
    k_blk = _tile_rows_masked(k_c, C_HEADS, ns, HEAD_DIM).astype(BF16)
    v_blk = _tile_rows_masked(v_c, C_HEADS, ns, HEAD_DIM).astype(BF16)
    att = _dot_nt(qc, k_blk) * dintra_ref[...]
    intra = _dot(att.astype(BF16), v_blk)
    s0 = s0_ref[...]
    q_exp = _tile_cols_masked(q_c, db, ts, C_WIDTH).astype(BF16)
    inter = _dot(q_exp, s0.astype(BF16)) * dq_ref[...]
    o_c = intra + inter
    kd_t = jnp.transpose(k_c * dk_ref[...])
    kd_exp = _tile_rows_masked(kd_t, db, C_WIDTH, ts).astype(BF16)
    sret_ref[...] = s0 * dch_ref[...] + _dot(kd_exp, v_c.astype(BF16))

    oa_ref[...] = (_head_norm(o_a, seg) * gma_ref[...]).astype(BF16)
    oc_ref[...] = (_head_norm(o_c, seg) * gmc_ref[...] * _silu(c_g)).astype(BF16)


def _proj_sample_call(x2, mod_rows, g, w, cst, s0_blk, *, ts, db):
    ns, d = x2.shape
    in_arrays = [
        x2, mod_rows[:, 3 * d:4 * d], mod_rows[:, 4 * d:5 * d], g,
        w["wa"], w["wb"], w["wf"], w["wc"],
        w["gav"], w["gq"], w["gk"], w["bf"], w["gma"], w["gmc"], cst["cos_s"], cst["sin_s"],
        w["wst"], w["bsp_s"], cst["dintra_s"], cst["dq_s"], cst["dk_s"], cst["dch_s"], cst["seg"], s0_blk,
    ]
    out_shape = [
        jax.ShapeDtypeStruct((ns, B_WIDTH), F32), jax.ShapeDtypeStruct((ns, B_WIDTH), F32),
        jax.ShapeDtypeStruct((ns, B_HEADS), F32), jax.ShapeDtypeStruct((ns, B_WIDTH), F32),
        jax.ShapeDtypeStruct((ns, A_WIDTH), F32),
        jax.ShapeDtypeStruct((ns, A_WIDTH), BF16), jax.ShapeDtypeStruct((ns, C_WIDTH), BF16),
        jax.ShapeDtypeStruct(s0_blk.shape, F32),
    ]
    return pl.pallas_call(
        functools.partial(_proj_sample_kernel, ns=ns, ts=ts, db=db),
        out_shape=out_shape,
        compiler_params=pltpu.CompilerParams(vmem_limit_bytes=VMEM_LIMIT),
        name="mix_proj_sample",
    )(*in_arrays)


def _fox_sample_kernel(pt_ref, q_ref, kn_ref, vn_ref, lfn_ref, triu_ref, *rest, ts, gp, page):
    kp_refs = rest[:gp]
    vp_refs = rest[gp:2 * gp]
    lf_refs = rest[2 * gp:3 * gp]
    o_ref = rest[3 * gp]
    qb_scr, qf_scr, m_scr, l_scr, acc_scr, cr_scr = rest[3 * gp + 1:]
    del pt_ref
    pg = pl.program_id(1)
    nrow = ts * B_HEADS
    hmask = (lax.broadcasted_iota(jnp.int32, (B_HEADS, B_WIDTH), 1) // HEAD_DIM
             == lax.broadcasted_iota(jnp.int32, (B_HEADS, B_WIDTH), 0))

    @pl.when(pg == 0)
    def _():
        q = q_ref[...]
        for t in range(ts):
            qt = jnp.where(hmask, jnp.broadcast_to(q[t:t + 1, :], (B_HEADS, B_WIDTH)), 0.0)
            qf_scr[t * B_HEADS:(t + 1) * B_HEADS, :] = qt
            qb_scr[t * B_HEADS:(t + 1) * B_HEADS, :] = qt.astype(BF16)
        m_scr[...] = jnp.full(m_scr.shape, NEG_BIG, F32)
        l_scr[...] = jnp.zeros_like(l_scr)
        acc_scr[...] = jnp.zeros_like(acc_scr)
        cr_scr[...] = jnp.zeros_like(cr_scr)

    qb = qb_scr[...]
    m = m_scr[...]
    l = l_scr[...]
    acc = acc_scr[...]
    carry = cr_scr[:, 0:1]
    lf_all = jnp.concatenate([lf_refs[g][...] for g in range(gp)], axis=0)
    local = _dot_split_lhs(lf_all, triu_ref[...], 3)
    bias = []
    for g in range(gp):
        cum_t = local[g * B_HEADS:(g + 1) * B_HEADS] + carry
        carry = cum_t[:, page - 1:page]
        bias.append(jnp.concatenate([cum_t] * ts, axis=0))
    k_all = jnp.concatenate([kp_refs[g][...].astype(BF16) for g in range(gp)], axis=1)
    s = _dot(qb, k_all) - jnp.concatenate(bias, axis=1)
    m_new = jnp.maximum(m, jnp.max(s, axis=-1, keepdims=True))
    p = jnp.exp(s - m_new)
    alpha = jnp.exp(m - m_new)
    l = alpha * l + jnp.sum(p, axis=-1, keepdims=True)
    v_all = jnp.concatenate([vp_refs[g][...].astype(BF16) for g in range(gp)], axis=1)
    acc = alpha * acc + _dot_nt(p.astype(BF16), v_all)
    m = m_new
    m_scr[...] = m
    l_scr[...] = l
    acc_scr[...] = acc
    cr_scr[...] = jnp.broadcast_to(carry, cr_scr.shape)

    @pl.when(pg == pl.num_programs(1) - 1)
    def _():
        qf = qf_scr[...]
        kn = kn_ref[...]
        vn = vn_ref[...]
        lfn = lfn_ref[...]
        tok = lax.broadcasted_iota(jnp.int32, (nrow, 1), 0) // B_HEADS
        run = carry
        s_new = []
        for jn in range(ts):
            run = run + lfn[:, jn:jn + 1]
            sj = jnp.sum(qf * kn[jn:jn + 1, :], axis=-1, keepdims=True) - jnp.concatenate([run] * ts, axis=0)
            s_new.append(jnp.where(tok >= jn, sj, NEG_BIG))
        m2 = m
        for sj in s_new:
            m2 = jnp.maximum(m2, sj)
        alpha = jnp.exp(m - m2)
        l2 = alpha * l
        acc2 = alpha * acc
        for jn in range(ts):
            pj = jnp.exp(s_new[jn] - m2)
            l2 = l2 + pj
            acc2 = acc2 + pj * vn[jn:jn + 1, :]
        o_full = acc2 / l2
        for t in range(ts):
            blk = jnp.where(hmask, o_full[t * B_HEADS:(t + 1) * B_HEADS, :], 0.0)
            o_ref[t:t + 1, :] = jnp.sum(blk, axis=0, keepdims=True)


def _fox_sample_call(page_table, q3, kn3, vn3, lfn_t, triu, kc, vc, lfc_t, layer, *, gp):
    db, ts, _ = q3.shape
    n_pages = page_table.shape[1]
    page = kc.shape[3]
    nrow = ts * B_HEADS
    seq3 = lambda shape: pl.BlockSpec((None,) + shape, lambda b, pg, pt: (b, 0, 0))

    def page_spec(shape, g):
        return pl.BlockSpec((None, None) + shape, lambda b, pg, pt, g=g: (layer, pt[b, pg * gp + g], 0, 0))

    in_specs = [seq3((ts, B_WIDTH)), seq3((ts, B_WIDTH)), seq3((ts, B_WIDTH)), seq3((B_HEADS, ts)),
                pl.BlockSpec((page, page), lambda b, pg, pt: (0, 0))]
    in_specs += [page_spec((B_WIDTH, page), g) for g in range(gp)]
    in_specs += [page_spec((B_WIDTH, page), g) for g in range(gp)]
    in_specs += [page_spec((B_HEADS, page), g) for g in range(gp)]
    grid_spec = pltpu.PrefetchScalarGridSpec(
        num_scalar_prefetch=1,
        grid=(db, n_pages // gp),
        in_specs=in_specs,
        out_specs=pl.BlockSpec((None, ts, B_WIDTH), lambda b, pg, pt: (b, 0, 0)),
        scratch_shapes=[
            pltpu.VMEM((nrow, B_WIDTH), BF16), pltpu.VMEM((nrow, B_WIDTH), F32),
            pltpu.VMEM((nrow, 1), F32), pltpu.VMEM((nrow, 1), F32),
            pltpu.VMEM((nrow, B_WIDTH), F32), pltpu.VMEM((B_HEADS, LANES), F32),
        ],
    )
    return pl.pallas_call(
        functools.partial(_fox_sample_kernel, ts=ts, gp=gp, page=page),
        grid_spec=grid_spec,
        out_shape=jax.ShapeDtypeStruct((db, ts, B_WIDTH), F32),
        compiler_params=pltpu.CompilerParams(
            dimension_semantics=("arbitrary", "arbitrary"), vmem_limit_bytes=VMEM_LIMIT),
        name="fox_sample",
    )(page_table, q3, kn3, vn3, lfn_t, triu, *([kc] * gp), *([vc] * gp), *([lfc_t] * gp))


def _rope_tables(pos):
    half = HEAD_DIM // 2
    inv = ROPE_BASE ** (-jnp.arange(half, dtype=F32) / half)
    ang = pos[:, None] * inv[None, :]
    cos = jnp.cos(ang)
    sin = jnp.sin(ang)
    cos_h = jnp.concatenate([cos, cos], axis=-1)
    sin_h = jnp.concatenate([-sin, sin], axis=-1)
    return jnp.tile(cos_h, (1, C_HEADS)), jnp.tile(sin_h, (1, C_HEADS))


def _decay_tables(length, log_gamma):
    idx = jnp.arange(length, dtype=F32)
    rel = idx[:, None] - idx[None, :]
    d_intra = jnp.where(rel[None] >= 0, jnp.exp(jnp.maximum(rel, 0.0)[None] * log_gamma[:, None, None]), 0.0)
    d_q = jnp.exp((idx + 1.0)[:, None] * log_gamma[None, :])
    d_k = jnp.exp((length - 1.0 - idx)[:, None] * log_gamma[None, :])
    d_chunk = jnp.exp(length * log_gamma)
    return d_intra, d_q, d_k, d_chunk


def _constants(t, ts, db, past, tm):
    log_gamma = jnp.log1p(-jnp.exp2(-5.0 - jnp.arange(C_HEADS, dtype=F32)))
    cst = {}
    cst["cos_p"], cst["sin_p"] = _rope_tables(jnp.arange(t, dtype=F32))
    cos_s, sin_s = _rope_tables(past + jnp.arange(ts, dtype=F32))
    cst["cos_s"], cst["sin_s"] = jnp.tile(cos_s, (db, 1)), jnp.tile(sin_s, (db, 1))
    rep = lambda a: jnp.repeat(a, HEAD_DIM, axis=-1)

    d_intra, d_q, d_k, d_chunk = _decay_tables(CHUNK, log_gamma)
    cst["dintra_p"] = jnp.concatenate([d_intra[h] for h in range(C_HEADS)], axis=1)
    cst["dq_p"], cst["dk_p"], cst["dch_p"] = rep(d_q), rep(d_k), rep(d_chunk[None, :])

    d_intra, d_q, d_k, d_chunk = _decay_tables(ts, log_gamma)
    same_seq = jnp.kron(jnp.eye(db, dtype=F32), jnp.ones((ts, ts), F32))
    cst["dintra_s"] = jnp.concatenate(
        [jnp.tile(d_intra[h], (db, db)) * same_seq for h in range(C_HEADS)], axis=1)
    cst["dq_s"] = jnp.tile(rep(d_q), (db, 1))
    cst["dk_s"] = jnp.tile(rep(d_k), (db, 1))
    cst["dch_s"] = rep(d_chunk[None, :])

    seg = jnp.kron(jnp.eye(MXU_DIM // HEAD_DIM, dtype=F32), jnp.ones((HEAD_DIM, HEAD_DIM), F32))
    cst["seg"] = seg.astype(BF16)
    cst["tril"] = jnp.tril(jnp.ones((tm, tm), F32)).astype(BF16)

    npiece = 3
    pk = np.zeros((npiece * LANES, (B_HEADS // 2) * LANES), np.float32)
    kconst = np.zeros((1, (B_HEADS // 2) * LANES), np.float32)
    pq = np.zeros((npiece * LANES, B_HEADS * LANES), np.float32)
    qconst = np.zeros((1, B_HEADS * LANES), np.float32)
    for h in range(B_HEADS):
        hp, i = divmod(h, 2)
        for a in range(npiece):
            kconst[0, hp * LANES + a] = 1.0
            pq[a * LANES + h, h * LANES + a] = 1.0
            pk[a * LANES + h, hp * LANES + npiece * (1 + i) + a] = 1.0
            qconst[0, h * LANES + npiece * (1 + i) + a] = -1.0
    cst["pk"], cst["pq"] = jnp.asarray(pk, BF16), jnp.asarray(pq, BF16)
    cst["kconst"], cst["qconst"] = jnp.asarray(kconst), jnp.asarray(qconst)
    return cst


def _layer_weights(l, w_in, g_a_v, w_spatial, b_spatial, g_qnorm, g_knorm, b_forget, g_mix, ts, db):
    d = w_in.shape[1]
    wl = w_in[l]
    o = 0
    wa = wl[:, o:o + 2 * A_WIDTH]; o += 2 * A_WIDTH
    wb = wl[:, o:o + 3 * B_WIDTH]; o += 3 * B_WIDTH
    wf = wl[:, o:o + B_HEADS]; o += B_HEADS
    wc = wl[:, o:o + 4 * C_WIDTH]
    w = {
        "wa": wa.astype(BF16), "wb": wb.astype(BF16), "wc": wc.astype(BF16),
        "wf": jnp.pad(wf, ((0, 0), (0, LANES - B_HEADS))).astype(BF16),
        "gav": g_a_v[l][None, :],
        "gq": jnp.tile(g_qnorm[l], B_HEADS)[None, :] * (HEAD_DIM ** -0.5),
        "gk": jnp.tile(g_knorm[l], B_HEADS)[None, :],
        "bf": jnp.pad(b_forget[l], (0, LANES - B_HEADS))[None, :],
        "gma": g_mix[l][None, :A_WIDTH],
        "gmb": g_mix[l][None, A_WIDTH:A_WIDTH + B_WIDTH],
        "gmc": g_mix[l][None, A_WIDTH + B_WIDTH:],
        "ws": w_spatial[l],
        "bsp_p": jnp.repeat(b_spatial[l].T, HEAD_DIM, axis=-1),
        "wst": jnp.tile(w_spatial[l][:, :ts, :ts], (1, db, db)),
        "bsp_s": jnp.tile(jnp.repeat(b_spatial[l][:, :ts].T, HEAD_DIM, axis=-1), (db, 1)),
    }
    del d
    return w


def kernel(x_prompt, x_sample, cache_k, cache_v, cache_logf, state_ret, page_table, c_prompt, c_sample,
           g_norm, w_ada, b_ada, w_ffn_gate, w_ffn_up, w_ffn_down, w_in, g_a_v, w_spatial, b_spatial,
           g_qnorm, g_knorm, b_forget, g_mix, w_out):
    b, t, d = x_prompt.shape
    db, ts, _ = x_sample.shape
    depth = w_in.shape[0]
    ns = db * ts
    n_pool, page = cache_k.shape[1], cache_k.shape[2]
    n_pages = page_table.shape[1]
    past = n_pages * page

    tm = min(512, t)
    ta = min(512, t)
    nf = 2
    gp = min(16, n_pages)

    cst = _constants(t, ts, db, past, tm)
    triu_page = jnp.triu(jnp.ones((page, page), F32)).astype(BF16)

    mod = _ada_call(jnp.concatenate([c_prompt, c_sample], axis=0), w_ada, b_ada)

    kc = jnp.transpose(cache_k, (0, 1, 3, 4, 2)).reshape(depth, n_pool, B_WIDTH, page)
    vc = jnp.transpose(cache_v, (0, 1, 3, 4, 2)).reshape(depth, n_pool, B_WIDTH, page)
    lfc_t = jnp.swapaxes(cache_logf, 2, 3)

    x_p = x_prompt.reshape(b * t, d)
    x_s = x_sample.reshape(ns, d)
    outs = {k: [] for k in ("kp", "vp", "lp", "rp", "ks", "vs", "ls", "rs", "av")}
    head_idx = jnp.arange(C_HEADS)
    for l in range(depth):
        mod_p = mod[l, :b].reshape(b, 1, -1)
        mod_s = jnp.repeat(mod[l, b:], ts, axis=0)
        w = _layer_weights(l, w_in, g_a_v, w_spatial, b_spatial, g_qnorm, g_knorm, b_forget, g_mix, ts, db)
        ffn_w = [(w_ffn_gate[l, i].astype(BF16), w_ffn_up[l, i].astype(BF16), w_ffn_down[l, i].astype(BF16))
                 for i in range(2)]
        w_out_l = w_out[l].astype(BF16)
        prompt_cfg = dict(per_row=False, tm=tm, seq_len=t, nf=nf)
        sample_cfg = dict(per_row=True, tm=ns, seq_len=ts, nf=nf)

        x_p = _ffn_call(x_p, mod_p, 0, g_norm[l, 0][None, :], *ffn_w[0], **prompt_cfg)
        x_s = _ffn_call(x_s, mod_s, 0, g_norm[l, 0][None, :], *ffn_w[0], **sample_cfg)

        (kt32, vt32, logf, qaug, kaug, vt16, oa, oc, sret) = _proj_prompt_call(
            x_p.reshape(b, t, d), mod_p, g_norm[l, 1][None, :], w, cst, tm=tm, ta=ta)
        ob = _fox_prompt_call(qaug, kaug, vt16, ta=ta, nh=FOX_HEADS_PER_STEP)
        merge_p = (oa.reshape(b * t, -1), ob.reshape(b * t, -1), oc.reshape(b * t, -1),
                   w["gmb"], cst["seg"], w_out_l)
        outs["kp"].append(jnp.transpose(kt32.reshape(b, B_HEADS, HEAD_DIM, t), (0, 3, 1, 2)))
        outs["vp"].append(jnp.transpose(vt32.reshape(b, B_HEADS, HEAD_DIM, t), (0, 3, 1, 2)))
        outs["lp"].append(logf)
        s5 = sret.reshape(b, C_HEADS, HEAD_DIM, C_HEADS, HEAD_DIM)
        outs["rp"].append(jnp.moveaxis(s5[:, head_idx, :, head_idx, :], 0, 1))

        s0_blk = jnp.einsum("bhde,hg->bhdge", state_ret[l], jnp.eye(C_HEADS, dtype=F32)).reshape(
            db * C_WIDTH, C_WIDTH)
        (k32s, v32s, logfs, q32s, avs, oas, ocs, srets) = _proj_sample_call(
            x_s, mod_s, g_norm[l, 1][None, :], w, cst, s0_blk, ts=ts, db=db)
        obs = _fox_sample_call(
            page_table, q32s.reshape(db, ts, -1), k32s.reshape(db, ts, -1), v32s.reshape(db, ts, -1),
            jnp.swapaxes(logfs.reshape(db, ts, B_HEADS), 1, 2), triu_page, kc, vc, lfc_t, l, gp=gp)
        merge_s = (oas, obs.reshape(ns, -1), ocs, w["gmb"], cst["seg"], w_out_l)
        outs["ks"].append(k32s.reshape(db, ts, B_HEADS, HEAD_DIM))
        outs["vs"].append(v32s.reshape(db, ts, B_HEADS, HEAD_DIM))
        outs["ls"].append(logfs.reshape(db, ts, B_HEADS))
        s5 = srets.reshape(db, C_HEADS, HEAD_DIM, C_HEADS, HEAD_DIM)
        outs["rs"].append(jnp.moveaxis(s5[:, head_idx, :, head_idx, :], 0, 1))
        outs["av"].append(avs.reshape(db, ts, A_WIDTH))

        x_p = _ffn_call(x_p, mod_p, 2, g_norm[l, 2][None, :], *ffn_w[1], merge=merge_p, **prompt_cfg)
        x_s = _ffn_call(x_s, mod_s, 2, g_norm[l, 2][None, :], *ffn_w[1], merge=merge_s, **sample_cfg)

    st = lambda key: jnp.stack(outs[key])
    return (x_p.reshape(b, t, d), x_s.reshape(db, ts, d), st("kp"), st("vp"), st("lp"), st("rp"),
            st("ks"), st("vs"), st("ls"), st("rs"), st("av"))
```

```python
import functools

import numpy as np
import jax
import jax.numpy as jnp
from jax import lax
from jax.experimental import pallas as pl
from jax.experimental.pallas import tpu as pltpu

F32 = jnp.float32
BF16 = jnp.bfloat16

HEAD_DIM = 64
A_HEADS, B_HEADS, C_HEADS = 4, 8, 4
A_WIDTH, B_WIDTH, C_WIDTH = A_HEADS * HEAD_DIM, B_HEADS * HEAD_DIM, C_HEADS * HEAD_DIM
CHUNK = 128
ROPE_BASE = 10000.0
EPS = 1e-6
N_SUBLAYERS = 3
LANES = 128
MXU_DIM = 256
NEG_BIG = -1e30
FOX_HEADS_PER_STEP = 8
SAMPLE_PAGE_GROUPS = 8
V_ROWS = 80
LOG2E = 1.4426950408889634
VMEM_LIMIT = 56 * 1024 * 1024


def _dot(a, b):
    return jnp.dot(a, b, preferred_element_type=F32)


def _dot_nt(a, b):
    return lax.dot_general(a, b, (((1,), (1,)), ((), ())), preferred_element_type=F32)


def _silu(x):
    return x / (1.0 + jnp.exp(-x))


def _gelu_tanh(x):
    c = 0.7978845608028654
    return 0.5 * x * (1.0 + jnp.tanh(c * (x + 0.044715 * (x * x * x))))


def _log_sigmoid(x):
    return jnp.minimum(x, 0.0) - jnp.log1p(jnp.exp(-jnp.abs(x)))


def _modnorm(x, g, shift, scale):
    ms = jnp.mean(x * x, axis=-1, keepdims=True)
    return (x * lax.rsqrt(ms + EPS) * g) * (1.0 + scale) + shift


def _split_bf16(x, parts):
    out = []
    r = x
    for i in range(parts):
        p = r.astype(BF16)
        out.append(p)
        if i + 1 < parts:
            r = r - p.astype(F32)
    return out


def _dot_split_lhs(a, b_bf16, parts):
    acc = None
    for p in _split_bf16(a, parts):
        t = _dot(p, b_bf16)
        acc = t if acc is None else acc + t
    return acc


def _dot_split_rhs(a_bf16, b, parts):
    acc = None
    for p in _split_bf16(b, parts):
        t = _dot(a_bf16, p)
        acc = t if acc is None else acc + t
    return acc


def _head_mean_sq(x, seg):
    w = x.shape[-1]
    outs = []
    for c in range(w // MXU_DIM):
        xs = x[:, c * MXU_DIM:(c + 1) * MXU_DIM]
        outs.append(_dot_split_lhs(xs * xs, seg, 2))
    ss = outs[0] if len(outs) == 1 else jnp.concatenate(outs, axis=-1)
    return ss * (1.0 / HEAD_DIM)


def _head_norm(x, seg):
    return x * lax.rsqrt(_head_mean_sq(x, seg) + EPS)


def _rope(x, cos, sin_signed):
    outs = []
    for c in range(x.shape[-1] // LANES):
        xs = x[:, c * LANES:(c + 1) * LANES]
        fwd = pltpu.roll(xs, HEAD_DIM // 2, axis=1)
        bwd = pltpu.roll(xs, LANES - HEAD_DIM // 2, axis=1)
        lane = lax.broadcasted_iota(jnp.int32, xs.shape, 1)
        outs.append(jnp.where((lane % HEAD_DIM) < HEAD_DIM // 2, bwd, fwd))
    partner = jnp.concatenate(outs, axis=-1)
    return x * cos + partner * sin_signed


def _tile_rows_masked(x, nrep, rows_per, cols_per):
    xt = jnp.concatenate([x] * nrep, axis=0)
    r = lax.broadcasted_iota(jnp.int32, xt.shape, 0) // rows_per
    c = lax.broadcasted_iota(jnp.int32, xt.shape, 1) // cols_per
    return jnp.where(r == c, xt, 0.0)


def _tile_cols_masked(x, nrep, rows_per, cols_per):
    xt = jnp.concatenate([x] * nrep, axis=1)
    r = lax.broadcasted_iota(jnp.int32, xt.shape, 0) // rows_per
    c = lax.broadcasted_iota(jnp.int32, xt.shape, 1) // cols_per
    return jnp.where(r == c, xt, 0.0)


def _ada_kernel(c_ref, w_ref, b_ref, o_ref):
    a = _silu(c_ref[...]).astype(BF16)
    o_ref[...] = _dot(a, w_ref[...].astype(BF16)) + b_ref[...]


def _ada_call(c_all, w_ada, b_ada):
    depth, d, ncol = w_ada.shape
    rows = c_all.shape[0]
    tn = 1024
    return pl.pallas_call(
        _ada_kernel,
        grid=(depth, ncol // tn),
        in_specs=[
            pl.BlockSpec((rows, d), lambda l, n: (0, 0)),
            pl.BlockSpec((None, d, tn), lambda l, n: (l, 0, n)),
            pl.BlockSpec((None, 1, tn), lambda l, n: (l, 0, n)),
        ],
        out_specs=pl.BlockSpec((None, rows, tn), lambda l, n: (l, 0, n)),
        out_shape=jax.ShapeDtypeStruct((depth, rows, ncol), F32),
        compiler_params=pltpu.CompilerParams(
            dimension_semantics=("arbitrary", "arbitrary"), vmem_limit_bytes=VMEM_LIMIT),
        name="adaln",
    )(c_all, w_ada, b_ada.reshape(depth, 1, ncol))


def _mod_specs(per_row, tm, d, tiles_per_seq, sub, which):
    col = sub * 3 + which
    if per_row:
        return pl.BlockSpec((tm, d), lambda i: (i, col))
    return pl.BlockSpec((None, 1, d), lambda i: (i // tiles_per_seq, 0, col))


def _half_ffn(x, shift, scale, gate_mod, g, wg_ref, wu_ref, wd_ref, nf):
    d_ff = wg_ref.shape[1]
    tf = d_ff // nf
    h = _modnorm(x, g, shift, scale).astype(BF16)
    acc = None
    for f in range(nf):
        cols = slice(f * tf, (f + 1) * tf)
        gate = _dot(h, wg_ref[:, cols])
        up = _dot(h, wu_ref[:, cols])
        a = (_silu(gate) * up).astype(BF16)
        t = _dot(a, wd_ref[cols, :])
        acc = t if acc is None else acc + t
    return x + (0.5 * gate_mod) * acc


def _ffn_kernel(x_ref, sh_ref, sc_ref, gt_ref, g_ref, wg_ref, wu_ref, wd_ref, o_ref, *, nf):
    o_ref[...] = _half_ffn(x_ref[...], sh_ref[...], sc_ref[...], gt_ref[...], g_ref[...],
                           wg_ref, wu_ref, wd_ref, nf)


def _merge_ffn_kernel(oa_ref, ob_ref, oc_ref, x_ref, gt1_ref, gmb_ref, seg_ref, wo_ref,
                      sh_ref, sc_ref, gt_ref, g_ref, wg_ref, wu_ref, wd_ref, o_ref, *, nf):
    ob = (_head_norm(ob_ref[...], seg_ref[...]) * gmb_ref[...]).astype(BF16)
    ab = A_WIDTH + B_WIDTH
    y = (_dot(oa_ref[...], wo_ref[:A_WIDTH, :]) + _dot(ob, wo_ref[A_WIDTH:ab, :])
         + _dot(oc_ref[...], wo_ref[ab:, :]))
    x = x_ref[...] + gt1_ref[...] * y
    o_ref[...] = _half_ffn(x, sh_ref[...], sc_ref[...], gt_ref[...], g_ref[...], wg_ref, wu_ref, wd_ref, nf)


def _ffn_call(x2, mod, sub, g, wg, wu, wd, *, per_row, tm, seq_len, nf, merge=None):
    n, d = x2.shape
    d_ff = wg.shape[1]
    tiles_per_seq = seq_len // tm if not per_row else 1
    const2 = lambda i: (0, 0)
    rows = lambda width: pl.BlockSpec((tm, width), lambda i: (i, 0))
    resident = lambda shape: pl.BlockSpec(shape, const2, pipeline_mode=pl.Buffered(1))
    ffn_specs = [
        _mod_specs(per_row, tm, d, tiles_per_seq, sub, 0),
        _mod_specs(per_row, tm, d, tiles_per_seq, sub, 1),
        _mod_specs(per_row, tm, d, tiles_per_seq, sub, 2),
        pl.BlockSpec((1, d), const2),
        resident((d, d_ff)), resident((d, d_ff)), resident((d_ff, d)),
    ]
    ffn_args = (mod, mod, mod, g, wg, wu, wd)
    if merge is None:
        body, in_specs, args = _ffn_kernel, [rows(d)] + ffn_specs, (x2,) + ffn_args
    else:
        oa, ob, oc, gmb, seg, w_out = merge
        body = _merge_ffn_kernel
        in_specs = [rows(A_WIDTH), rows(B_WIDTH), rows(C_WIDTH), rows(d),
                    _mod_specs(per_row, tm, d, tiles_per_seq, 1, 2),
                    pl.BlockSpec((1, B_WIDTH), const2), pl.BlockSpec((MXU_DIM, MXU_DIM), const2),
                    resident((d, d))] + ffn_specs
        args = (oa, ob, oc, x2, mod, gmb, seg, w_out) + ffn_args
    return pl.pallas_call(
        functools.partial(body, nf=nf),
        grid=(n // tm,),
        in_specs=in_specs,
        out_specs=rows(d),
        out_shape=jax.ShapeDtypeStruct((n, d), F32),
        compiler_params=pltpu.CompilerParams(
            dimension_semantics=("arbitrary",), vmem_limit_bytes=VMEM_LIMIT),
        name="half_ffn" if merge is None else "merge_ffn",
    )(*args)


def _proj_prompt_kernel(
        x_ref, sh_ref, sc_ref, g_ref, wa_ref, wb_ref, wf_ref, wc_ref,
        gav_ref, gq_ref, gk_ref, bf_ref, gma_ref, gmc_ref, cos_ref, sin_ref,
        ws_ref, bsp_ref, dintra_ref, dq_ref, dk_ref, dch_ref, seg_ref, tril_ref,
        pk_ref, kconst_ref, pq_ref, qconst_ref,
        kt32_ref, vt32_ref, logf_ref, qaug_ref, kaug_ref, vt16_ref,
        oa_ref, oc_ref, sret_ref,
        s_scr, cc_scr, *, tm, ta):
    j = pl.program_id(1)

    @pl.when(j == 0)
    def _():
        s_scr[...] = jnp.zeros_like(s_scr)
        cc_scr[...] = jnp.zeros_like(cc_scr)

    x = x_ref[...]
    hb = _modnorm(x, g_ref[...], sh_ref[...], sc_ref[...]).astype(BF16)
    seg = seg_ref[...]

    za = _dot(hb, wa_ref[...])
    zb = _dot(hb, wb_ref[...])
    zf = _dot(hb, wf_ref[...])
    zc = _dot(hb, wc_ref[...])

    a_u = _gelu_tanh(za[:, :A_WIDTH])
    a_vr = _gelu_tanh(za[:, A_WIDTH:])
    a_v = a_vr * lax.rsqrt(jnp.mean(a_vr * a_vr, axis=-1, keepdims=True) + EPS) * gav_ref[...]

    q_b = _head_norm(zb[:, :B_WIDTH], seg) * gq_ref[...]
    k_b = _head_norm(zb[:, B_WIDTH:2 * B_WIDTH], seg) * gk_ref[...]
    v_b = zb[:, 2 * B_WIDTH:]
    kt32_ref[...] = jnp.transpose(k_b)
    v_t = jnp.transpose(v_b)
    vt32_ref[...] = v_t
    v_t16 = v_t.astype(BF16)
    pad_row = lax.broadcasted_iota(jnp.int32, (V_ROWS - HEAD_DIM, ta), 0)
    ones_pad = jnp.where(pad_row == 0, 1.0, 0.0).astype(BF16)
    for h in range(B_HEADS):
        for c in range(tm // ta):
            vt16_ref[h, c] = jnp.concatenate(
                [v_t16[h * HEAD_DIM:(h + 1) * HEAD_DIM, c * ta:(c + 1) * ta], ones_pad], axis=0)

    logf = _log_sigmoid(zf + bf_ref[...])
    logf_ref[...] = logf[:, :B_HEADS]
    cum_c = _dot_split_rhs(tril_ref[...], logf, 3) + cc_scr[...]
    cc_scr[...] = cum_c[tm - 1:tm, :]
    pieces = jnp.concatenate(_split_bf16(cum_c * LOG2E, 3), axis=-1)
    k_aux = (_dot(pieces, pk_ref[...]) + kconst_ref[...]).astype(BF16)
    q_aux = (_dot(pieces, pq_ref[...]) + qconst_ref[...]).astype(BF16)
    q16 = (q_b * LOG2E).astype(BF16)
    k16 = k_b.astype(BF16)
    lane = lax.broadcasted_iota(jnp.int32, (tm, LANES), 1)
    for hp in range(B_HEADS // 2):
        pair = slice(hp * LANES, (hp + 1) * LANES)
        kaug_ref[hp, :, :LANES] = k16[:, pair]
        kaug_ref[hp, :, LANES:] = k_aux[:, pair]
        for i in range(2):
            h = 2 * hp + i
            own = (lane < HEAD_DIM) if i == 0 else (lane >= HEAD_DIM)
            qaug_ref[h, :, :LANES] = jnp.where(own, q16[:, pair], jnp.zeros_like(q16[:, pair]))
            qaug_ref[h, :, LANES:] = q_aux[:, h * LANES:(h + 1) * LANES]

    cos = cos_ref[...]
    sin = sin_ref[...]
    q_c = _rope(zc[:, :C_WIDTH], cos, sin)
    k_c = _rope(zc[:, C_WIDTH:2 * C_WIDTH], cos, sin) * (HEAD_DIM ** -0.5)
    v_c = zc[:, 2 * C_WIDTH:3 * C_WIDTH]
    c_g = zc[:, 3 * C_WIDTH:]

    row = lax.broadcasted_iota(jnp.int32, (CHUNK, CHUNK), 0)
    col = lax.broadcasted_iota(jnp.int32, (CHUNK, CHUNK), 1)
    w_cat = jnp.concatenate(
        [jnp.where(col <= row, ws_ref[h], 0.0) for h in range(A_HEADS)], axis=1).astype(BF16)
    dintra = dintra_ref[...]
    dq = dq_ref[...]
    dk = dk_ref[...]
    dch = dch_ref[...]
    bsp = bsp_ref[...]
    sr = lax.broadcasted_iota(jnp.int32, (C_WIDTH, C_WIDTH), 0) // HEAD_DIM
    scol = lax.broadcasted_iota(jnp.int32, (C_WIDTH, C_WIDTH), 1) // HEAD_DIM
    s_mask = sr == scol
    s_blk = s_scr[...]
    oa_parts = []
    oc_parts = []
    for c in range(tm // CHUNK):
        sl = slice(c * CHUNK, (c + 1) * CHUNK)
        av_blk = _tile_rows_masked(a_v[sl], A_HEADS, CHUNK, HEAD_DIM).astype(BF16)
        oa_parts.append(a_u[sl] * (_dot(w_cat, av_blk) + bsp))

        qc = q_c[sl].astype(BF16)
        k_blk = _tile_rows_masked(k_c[sl], C_HEADS, CHUNK, HEAD_DIM).astype(BF16)
        v_blk = _tile_rows_masked(v_c[sl], C_HEADS, CHUNK, HEAD_DIM).astype(BF16)
        att = _dot_nt(qc, k_blk) * dintra
        intra = _dot(att.astype(BF16), v_blk)
        inter = _dot(qc, s_blk.astype(BF16)) * dq
        oc_parts.append(intra + inter)
        kd_t = jnp.transpose(k_c[sl] * dk).astype(BF16)
        upd = _dot(kd_t, v_c[sl].astype(BF16))
        s_blk = s_blk * dch + jnp.where(s_mask, upd, 0.0)
    s_scr[...] = s_blk
    sret_ref[...] = s_blk

    o_a = oa_parts[0] if len(oa_parts) == 1 else jnp.concatenate(oa_parts, axis=0)
    o_c = oc_parts[0] if len(oc_parts) == 1 else jnp.concatenate(oc_parts, axis=0)
    oa_ref[...] = (_head_norm(o_a, seg) * gma_ref[...]).astype(BF16)
    oc_ref[...] = (_head_norm(o_c, seg) * gmc_ref[...] * _silu(c_g)).astype(BF16)


def _proj_prompt_call(x3, mod, g, w, cst, *, tm, ta):
    b, t, d = x3.shape
    nt = t // tm
    full2 = lambda bb, j: (0, 0)
    full3 = lambda bb, j: (0, 0, 0)
    row_spec = lambda width: pl.BlockSpec((None, tm, width), lambda bb, j: (bb, j, 0))
    mod_spec = lambda col: pl.BlockSpec((None, 1, d), lambda bb, j: (bb, 0, col))
    vec = lambda width: pl.BlockSpec((1, width), full2)
    in_specs = [
        row_spec(d), mod_spec(3), mod_spec(4), vec(d),
        pl.BlockSpec((d, 2 * A_WIDTH), full2), pl.BlockSpec((d, 3 * B_WIDTH), full2),
        pl.BlockSpec((d, LANES), full2), pl.BlockSpec((d, 4 * C_WIDTH), full2),
        vec(A_WIDTH), vec(B_WIDTH), vec(B_WIDTH), vec(LANES), vec(A_WIDTH), vec(C_WIDTH),
        pl.BlockSpec((tm, C_WIDTH), lambda bb, j: (j, 0)),
        pl.BlockSpec((tm, C_WIDTH), lambda bb, j: (j, 0)),
        pl.BlockSpec((A_HEADS, CHUNK, CHUNK), full3),
        pl.BlockSpec((CHUNK, A_WIDTH), full2),
        pl.BlockSpec((CHUNK, C_HEADS * CHUNK), full2),
        pl.BlockSpec((CHUNK, C_WIDTH), full2), pl.BlockSpec((CHUNK, C_WIDTH), full2),
        vec(C_WIDTH),
        pl.BlockSpec((MXU_DIM, MXU_DIM), full2),
        pl.BlockSpec((tm, tm), full2),
        pl.BlockSpec((3 * LANES, (B_HEADS // 2) * LANES), full2), vec((B_HEADS // 2) * LANES),
        pl.BlockSpec((3 * LANES, B_HEADS * LANES), full2), vec(B_HEADS * LANES),
    ]
    nhp = B_HEADS // 2
    col_spec = pl.BlockSpec((None, B_WIDTH, tm), lambda bb, j: (bb, 0, j))
    out_shape = [
        jax.ShapeDtypeStruct((b, B_WIDTH, t), F32), jax.ShapeDtypeStruct((b, B_WIDTH, t), F32),
        jax.ShapeDtypeStruct((b, t, B_HEADS), F32),
        jax.ShapeDtypeStruct((b, B_HEADS, t, 2 * LANES), BF16),
        jax.ShapeDtypeStruct((b, nhp, t, 2 * LANES), BF16),
        jax.ShapeDtypeStruct((b, B_HEADS, t // ta, V_ROWS, ta), BF16),
        jax.ShapeDtypeStruct((b, t, A_WIDTH), BF16), jax.ShapeDtypeStruct((b, t, C_WIDTH), BF16),
        jax.ShapeDtypeStruct((b, C_WIDTH, C_WIDTH), F32),
    ]
    out_specs = [
        col_spec, col_spec, row_spec(B_HEADS),
        pl.BlockSpec((None, B_HEADS, tm, 2 * LANES), lambda bb, j: (bb, 0, j, 0)),
        pl.BlockSpec((None, nhp, tm, 2 * LANES), lambda bb, j: (bb, 0, j, 0)),
        pl.BlockSpec((None, B_HEADS, tm // ta, V_ROWS, ta), lambda bb, j: (bb, 0, j, 0, 0)),
        row_spec(A_WIDTH), row_spec(C_WIDTH),
        pl.BlockSpec((None, C_WIDTH, C_WIDTH), lambda bb, j: (bb, 0, 0)),
    ]
    return pl.pallas_call(
        functools.partial(_proj_prompt_kernel, tm=tm, ta=ta),
        grid=(b, nt),
        in_specs=in_specs,
        out_specs=out_specs,
        out_shape=out_shape,
        scratch_shapes=[
            pltpu.VMEM((C_WIDTH, C_WIDTH), F32),
            pltpu.VMEM((1, LANES), F32),
        ],
        compiler_params=pltpu.CompilerParams(
            dimension_semantics=("arbitrary", "arbitrary"), vmem_limit_bytes=VMEM_LIMIT),
        name="mix_proj_prompt",
    )(x3, mod, mod, g, w["wa"], w["wb"], w["wf"], w["wc"],
      w["gav"], w["gq"], w["gk"], w["bf"], w["gma"], w["gmc"], cst["cos_p"], cst["sin_p"],
      w["ws"], w["bsp_p"], cst["dintra_p"], cst["dq_p"], cst["dk_p"], cst["dch_p"],
      cst["seg"], cst["tril"], cst["pk"], cst["kconst"], cst["pq"], cst["qconst"])


def _fox_prompt_kernel(q_ref, k_ref, vt_ref, o_ref, *, ta, nh):
    j = pl.program_id(2)
    q_heads = [q_ref[h] for h in range(nh)]
    kpos = lax.broadcasted_iota(jnp.int32, (ta, ta), 0)
    qpos = lax.broadcasted_iota(jnp.int32, (ta, ta), 1)
    causal = kpos <= qpos

    def step(kb, carry, masked):
        start = pl.multiple_of(kb * ta, ta)
        kblks = [k_ref[hp, pl.ds(start, ta), :] for hp in range(nh // 2)]
        scores = [_dot_nt(kblks[h // 2], q_heads[h]) for h in range(nh)]
        new = []
        for h, (m, acc) in enumerate(carry):
            s = scores[h]
            if masked:
                s = jnp.where(causal, s, NEG_BIG)
            m_new = jnp.maximum(m, jnp.max(s, axis=0, keepdims=True))
            p = jnp.exp2(s - m_new)
            alpha = jnp.exp2(m - m_new)
            acc = alpha * acc + _dot(vt_ref[h, kb], p.astype(BF16))
            new.append((m_new, acc))
        return tuple(new)

    init = tuple((jnp.full((1, ta), NEG_BIG, F32), jnp.zeros((V_ROWS, ta), F32)) for _ in range(nh))
    carry = lax.fori_loop(0, j, functools.partial(step, masked=False), init)
    carry = step(j, carry, True)
    o_t = jnp.concatenate([acc[:HEAD_DIM] / acc[HEAD_DIM:HEAD_DIM + 1] for (_, acc) in carry], axis=0)
    o_ref[...] = jnp.transpose(o_t)


def _fox_prompt_call(qaug, kaug, vt16, *, ta, nh):
    b, _, t, wq = qaug.shape
    return pl.pallas_call(
        functools.partial(_fox_prompt_kernel, ta=ta, nh=nh),
        grid=(b, B_HEADS // nh, t // ta),
        in_specs=[
            pl.BlockSpec((None, nh, ta, wq), lambda bb, hg, j: (bb, hg, j, 0)),
            pl.BlockSpec((None, nh // 2, t, wq), lambda bb, hg, j: (bb, hg, 0, 0)),
            pl.BlockSpec((None, nh, t // ta, V_ROWS, ta), lambda bb, hg, j: (bb, hg, 0, 0, 0)),
        ],
        out_specs=pl.BlockSpec((None, ta, nh * HEAD_DIM), lambda bb, hg, j: (bb, j, hg)),
        out_shape=jax.ShapeDtypeStruct((b, t, B_WIDTH), F32),
        compiler_params=pltpu.CompilerParams(
            dimension_semantics=("arbitrary", "arbitrary", "arbitrary"), vmem_limit_bytes=VMEM_LIMIT),
        name="fox_prompt",
    )(qaug, kaug, vt16)


def _proj_sample_kernel(
        x_ref, sh_ref, sc_ref, g_ref, wa_ref, wb_ref, wf_ref, wc_ref,
        gav_ref, gq_ref, gk_ref, bf_ref, gma_ref, gmc_ref, cos_ref, sin_ref,
        wst_ref, bsp_ref, dintra_ref, dq_ref, dk_ref, dch_ref, seg_ref, s0_ref,
        k32_ref, v32_ref, logf_ref, q32_ref, av_ref, oa_ref, oc_ref, sret_ref, *, ns, ts, db):
    x = x_ref[...]
    hb = _modnorm(x, g_ref[...], sh_ref[...], sc_ref[...]).astype(BF16)
    seg = seg_ref[...]

    za = _dot(hb, wa_ref[...])
    a_u = _gelu_tanh(za[:, :A_WIDTH])
    a_vr = _gelu_tanh(za[:, A_WIDTH:])
    a_v = a_vr * lax.rsqrt(jnp.mean(a_vr * a_vr, axis=-1, keepdims=True) + EPS) * gav_ref[...]
    av_ref[...] = a_v

    zb = _dot(hb, wb_ref[...])
    q32_ref[...] = _head_norm(zb[:, :B_WIDTH], seg) * gq_ref[...]
    k32_ref[...] = _head_norm(zb[:, B_WIDTH:2 * B_WIDTH], seg) * gk_ref[...]
    v32_ref[...] = zb[:, 2 * B_WIDTH:]
    logf = _log_sigmoid(_dot(hb, wf_ref[...]) + bf_ref[...])
    logf_ref[...] = logf[:, :B_HEADS]

    zc = _dot(hb, wc_ref[...])
    cos = cos_ref[...]
    sin = sin_ref[...]
    q_c = _rope(zc[:, :C_WIDTH], cos, sin)
    k_c = _rope(zc[:, C_WIDTH:2 * C_WIDTH], cos, sin) * (HEAD_DIM ** -0.5)
    v_c = zc[:, 2 * C_WIDTH:3 * C_WIDTH]
    c_g = zc[:, 3 * C_WIDTH:]

    row = lax.broadcasted_iota(jnp.int32, (ns, ns), 0)
    col = lax.broadcasted_iota(jnp.int32, (ns, ns), 1)
    keep = (row // ts == col // ts) & (col <= row)
    w_cat = jnp.concatenate(
        [jnp.where(keep, wst_ref[h], 0.0) for h in range(A_HEADS)], axis=1).astype(BF16)
    av_blk = _tile_rows_masked(a_v, A_HEADS, ns, HEAD_DIM).astype(BF16)
    o_a = a_u * (_dot(w_cat, av_blk) + bsp_ref[...])

    qc = q_c.astype(BF16)
    k_blk = _tile_rows_masked(k_c, C_HEADS, ns, HEAD_DIM).astype(BF16)
    v_blk = _tile_rows_masked(v_c, C_HEADS, ns, HEAD_DIM).astype(BF16)
    att = _dot_nt(qc, k_blk) * dintra_ref[...]
    intra = _dot(att.astype(BF16), v_blk)
    s0 = s0_ref[...]
    q_exp = _tile_cols_masked(q_c, db, ts, C_WIDTH).astype(BF16)
    inter = _dot(q_exp, s0.astype(BF16)) * dq_ref[...]
    o_c = intra + inter
    kd_t = jnp.transpose(k_c * dk_ref[...])
    kd_exp = _tile_rows_masked(kd_t, db, C_WIDTH, ts).astype(BF16)
    sret_ref[...] = s0 * dch_ref[...] + _dot(kd_exp, v_c.astype(BF16))

    oa_ref[...] = (_head_norm(o_a, seg) * gma_ref[...]).astype(BF16)
    oc_ref[...] = (_head_norm(o_c, seg) * gmc_ref[...] * _silu(c_g)).astype(BF16)


def _proj_sample_call(x2, mod_rows, g, w, cst, s0_blk, *, ts, db):
    ns, d = x2.shape
    in_arrays = [
        x2, mod_rows[:, 3 * d:4 * d], mod_rows[:, 4 * d:5 * d], g,
        w["wa"], w["wb"], w["wf"], w["wc"],
        w["gav"], w["gq"], w["gk"], w["bf"], w["gma"], w["gmc"], cst["cos_s"], cst["sin_s"],
        w["wst"], w["bsp_s"], cst["dintra_s"], cst["dq_s"], cst["dk_s"], cst["dch_s"], cst["seg"], s0_blk,
    ]
    out_shape = [
        jax.ShapeDtypeStruct((ns, B_WIDTH), F32), jax.ShapeDtypeStruct((ns, B_WIDTH), F32),
        jax.ShapeDtypeStruct((ns, B_HEADS), F32), jax.ShapeDtypeStruct((ns, B_WIDTH), F32),
        jax.ShapeDtypeStruct((ns, A_WIDTH), F32),
        jax.ShapeDtypeStruct((ns, A_WIDTH), BF16), jax.ShapeDtypeStruct((ns, C_WIDTH), BF16),
        jax.ShapeDtypeStruct(s0_blk.shape, F32),
    ]
    return pl.pallas_call(
        functools.partial(_proj_sample_kernel, ns=ns, ts=ts, db=db),
        out_shape=out_shape,
        compiler_params=pltpu.CompilerParams(vmem_limit_bytes=VMEM_LIMIT),
        name="mix_proj_sample",
    )(*in_arrays)


def _fox_sample_kernel(pt_ref, q_ref, kn_ref, vn_ref, lfn_ref, triu_ref, *rest, ts, gp, page):
    kp_refs = rest[:gp]
    vp_refs = rest[gp:2 * gp]
    lf_ref = rest[2 * gp]
    o_ref = rest[2 * gp + 1]
    qb_scr, qf_scr, m_scr, l_scr, acc_scr, cr_scr = rest[2 * gp + 2:]
    pg = pl.program_id(1)
    seq = pl.program_id(0)
    nrow = ts * B_HEADS
    hmask = (lax.broadcasted_iota(jnp.int32, (B_HEADS, B_WIDTH), 1) // HEAD_DIM
             == lax.broadcasted_iota(jnp.int32, (B_HEADS, B_WIDTH), 0))

    @pl.when(pg == 0)
    def _():
        q = q_ref[...]
        for t in range(ts):
            qt = jnp.where(hmask, jnp.broadcast_to(q[t:t + 1, :], (B_HEADS, B_WIDTH)), 0.0)
            qf_scr[t * B_HEADS:(t + 1) * B_HEADS, :] = qt
            qb_scr[t * B_HEADS:(t + 1) * B_HEADS, :] = qt.astype(BF16)
        m_scr[...] = jnp.full(m_scr.shape, NEG_BIG, F32)
        l_scr[...] = jnp.zeros_like(l_scr)
        acc_scr[...] = jnp.zeros_like(acc_scr)
        cr_scr[...] = jnp.zeros_like(cr_scr)

    qb = qb_scr[...]
    m = m_scr[...]
    l = l_scr[...]
    acc = acc_scr[...]
    carry = cr_scr[:, 0:1]
    lf_all = jnp.concatenate([lf_ref[pt_ref[seq, pg * gp + g]] for g in range(gp)], axis=0)
    local = _dot_split_lhs(lf_all, triu_ref[...], 3)
    bias = []
    for g in range(gp):
        cum_t = local[g * B_HEADS:(g + 1) * B_HEADS] + carry
        carry = cum_t[:, page - 1:page]
        bias.append(jnp.concatenate([cum_t] * ts, axis=0))
    per = gp // SAMPLE_PAGE_GROUPS
    groups = [range(i * per, (i + 1) * per) for i in range(SAMPLE_PAGE_GROUPS)]
    scores = []
    for grp in groups:
        k_grp = jnp.concatenate([kp_refs[g][...].astype(BF16) for g in grp], axis=1)
        scores.append(_dot(qb, k_grp) - jnp.concatenate([bias[g] for g in grp], axis=1))
    for grp, s in zip(groups, scores):
        m_new = jnp.maximum(m, jnp.max(s, axis=-1, keepdims=True))
        p = jnp.exp(s - m_new)
        alpha = jnp.exp(m - m_new)
        l = alpha * l + jnp.sum(p, axis=-1, keepdims=True)
        v_grp = jnp.concatenate([vp_refs[g][...].astype(BF16) for g in grp], axis=1)
        acc = alpha * acc + _dot_nt(p.astype(BF16), v_grp)
        m = m_new
    m_scr[...] = m
    l_scr[...] = l
    acc_scr[...] = acc
    cr_scr[...] = jnp.broadcast_to(carry, cr_scr.shape)

    @pl.when(pg == pl.num_programs(1) - 1)
    def _():
        qf = qf_scr[...]
        kn = kn_ref[...]
        vn = vn_ref[...]
        lfn = lfn_ref[...]
        tok = lax.broadcasted_iota(jnp.int32, (nrow, 1), 0) // B_HEADS
        run = carry
        s_new = []
        for jn in range(ts):
            run = run + lfn[:, jn:jn + 1]
            sj = jnp.sum(qf * kn[jn:jn + 1, :], axis=-1, keepdims=True) - jnp.concatenate([run] * ts, axis=0)
            s_new.append(jnp.where(tok >= jn, sj, NEG_BIG))
        m2 = m
        for sj in s_new:
            m2 = jnp.maximum(m2, sj)
        alpha = jnp.exp(m - m2)
        l2 = alpha * l
        acc2 = alpha * acc
        for jn in range(ts):
            pj = jnp.exp(s_new[jn] - m2)
            l2 = l2 + pj
            acc2 = acc2 + pj * vn[jn:jn + 1, :]
        o_full = acc2 / l2
        for t in range(ts):
            blk = jnp.where(hmask, o_full[t * B_HEADS:(t + 1) * B_HEADS, :], 0.0)
            o_ref[t:t + 1, :] = jnp.sum(blk, axis=0, keepdims=True)


def _fox_sample_call(page_table, q3, kn3, vn3, lfn_t, triu, kc, vc, lfc_t, layer, *, gp):
    db, ts, _ = q3.shape
    n_pages = page_table.shape[1]
    page = kc.shape[3]
    nrow = ts * B_HEADS
    seq3 = lambda shape: pl.BlockSpec((None,) + shape, lambda b, pg, pt: (b, 0, 0))

    def page_spec(shape, g):
        return pl.BlockSpec((None, None) + shape, lambda b, pg, pt, g=g: (layer, pt[b, pg * gp + g], 0, 0))

    in_specs = [seq3((ts, B_WIDTH)), seq3((ts, B_WIDTH)), seq3((ts, B_WIDTH)), seq3((B_HEADS, ts)),
                pl.BlockSpec((page, page), lambda b, pg, pt: (0, 0))]
    in_specs += [page_spec((B_WIDTH, page), g) for g in range(gp)]
    in_specs += [page_spec((B_WIDTH, page), g) for g in range(gp)]
    in_specs += [pl.BlockSpec((None,) + lfc_t.shape[1:], lambda b, pg, pt: (layer, 0, 0, 0),
                              pipeline_mode=pl.Buffered(1))]
    grid_spec = pltpu.PrefetchScalarGridSpec(
        num_scalar_prefetch=1,
        grid=(db, n_pages // gp),
        in_specs=in_specs,
        out_specs=pl.BlockSpec((None, ts, B_WIDTH), lambda b, pg, pt: (b, 0, 0)),
        scratch_shapes=[
            pltpu.VMEM((nrow, B_WIDTH), BF16), pltpu.VMEM((nrow, B_WIDTH), F32),
            pltpu.VMEM((nrow, 1), F32), pltpu.VMEM((nrow, 1), F32),
            pltpu.VMEM((nrow, B_WIDTH), F32), pltpu.VMEM((B_HEADS, LANES), F32),
        ],
    )
    return pl.pallas_call(
        functools.partial(_fox_sample_kernel, ts=ts, gp=gp, page=page),
        grid_spec=grid_spec,
        out_shape=jax.ShapeDtypeStruct((db, ts, B_WIDTH), F32),
        compiler_params=pltpu.CompilerParams(
            dimension_semantics=("arbitrary", "arbitrary"), vmem_limit_bytes=VMEM_LIMIT),
        name="fox_sample",
    )(page_table, q3, kn3, vn3, lfn_t, triu, *([kc] * gp), *([vc] * gp), lfc_t)


def _rope_tables(pos):
    half = HEAD_DIM // 2
    inv = ROPE_BASE ** (-jnp.arange(half, dtype=F32) / half)
    ang = pos[:, None] * inv[None, :]
    cos = jnp.cos(ang)
    sin = jnp.sin(ang)
    cos_h = jnp.concatenate([cos, cos], axis=-1)
    sin_h = jnp.concatenate([-sin, sin], axis=-1)
    return jnp.tile(cos_h, (1, C_HEADS)), jnp.tile(sin_h, (1, C_HEADS))


def _decay_tables(length, log_gamma):
    idx = jnp.arange(length, dtype=F32)
    rel = idx[:, None] - idx[None, :]
    d_intra = jnp.where(rel[None] >= 0, jnp.exp(jnp.maximum(rel, 0.0)[None] * log_gamma[:, None, None]), 0.0)
    d_q = jnp.exp((idx + 1.0)[:, None] * log_gamma[None, :])
    d_k = jnp.exp((length - 1.0 - idx)[:, None] * log_gamma[None, :])
    d_chunk = jnp.exp(length * log_gamma)
    return d_intra, d_q, d_k, d_chunk


def _constants(t, ts, db, past, tm):
    log_gamma = jnp.log1p(-jnp.exp2(-5.0 - jnp.arange(C_HEADS, dtype=F32)))
    cst = {}
    cst["cos_p"], cst["sin_p"] = _rope_tables(jnp.arange(t, dtype=F32))
    cos_s, sin_s = _rope_tables(past + jnp.arange(ts, dtype=F32))
    cst["cos_s"], cst["sin_s"] = jnp.tile(cos_s, (db, 1)), jnp.tile(sin_s, (db, 1))
    rep = lambda a: jnp.repeat(a, HEAD_DIM, axis=-1)

    d_intra, d_q, d_k, d_chunk = _decay_tables(CHUNK, log_gamma)
    cst["dintra_p"] = jnp.concatenate([d_intra[h] for h in range(C_HEADS)], axis=1)
    cst["dq_p"], cst["dk_p"], cst["dch_p"] = rep(d_q), rep(d_k), rep(d_chunk[None, :])

    d_intra, d_q, d_k, d_chunk = _decay_tables(ts, log_gamma)
    same_seq = jnp.kron(jnp.eye(db, dtype=F32), jnp.ones((ts, ts), F32))
    cst["dintra_s"] = jnp.concatenate(
        [jnp.tile(d_intra[h], (db, db)) * same_seq for h in range(C_HEADS)], axis=1)
    cst["dq_s"] = jnp.tile(rep(d_q), (db, 1))
    cst["dk_s"] = jnp.tile(rep(d_k), (db, 1))
    cst["dch_s"] = rep(d_chunk[None, :])

    seg = jnp.kron(jnp.eye(MXU_DIM // HEAD_DIM, dtype=F32), jnp.ones((HEAD_DIM, HEAD_DIM), F32))
    cst["seg"] = seg.astype(BF16)
    cst["tril"] = jnp.tril(jnp.ones((tm, tm), F32)).astype(BF16)

    npiece = 3
    pk = np.zeros((npiece * LANES, (B_HEADS // 2) * LANES), np.float32)
    kconst = np.zeros((1, (B_HEADS // 2) * LANES), np.float32)
    pq = np.zeros((npiece * LANES, B_HEADS * LANES), np.float32)
    qconst = np.zeros((1, B_HEADS * LANES), np.float32)
    for h in range(B_HEADS):
        hp, i = divmod(h, 2)
        for a in range(npiece):
            kconst[0, hp * LANES + a] = 1.0
            pq[a * LANES + h, h * LANES + a] = 1.0
            pk[a * LANES + h, hp * LANES + npiece * (1 + i) + a] = 1.0
            qconst[0, h * LANES + npiece * (1 + i) + a] = -1.0
    cst["pk"], cst["pq"] = jnp.asarray(pk, BF16), jnp.asarray(pq, BF16)
    cst["kconst"], cst["qconst"] = jnp.asarray(kconst), jnp.asarray(qconst)
    return cst


def _layer_weights(l, w_in, g_a_v, w_spatial, b_spatial, g_qnorm, g_knorm, b_forget, g_mix, ts, db):
    d = w_in.shape[1]
    wl = w_in[l]
    o = 0
    wa = wl[:, o:o + 2 * A_WIDTH]; o += 2 * A_WIDTH
    wb = wl[:, o:o + 3 * B_WIDTH]; o += 3 * B_WIDTH
    wf = wl[:, o:o + B_HEADS]; o += B_HEADS
    wc = wl[:, o:o + 4 * C_WIDTH]
    w = {
        "wa": wa.astype(BF16), "wb": wb.astype(BF16), "wc": wc.astype(BF16),
        "wf": jnp.pad(wf, ((0, 0), (0, LANES - B_HEADS))).astype(BF16),
        "gav": g_a_v[l][None, :],
        "gq": jnp.tile(g_qnorm[l], B_HEADS)[None, :] * (HEAD_DIM ** -0.5),
        "gk": jnp.tile(g_knorm[l], B_HEADS)[None, :],
        "bf": jnp.pad(b_forget[l], (0, LANES - B_HEADS))[None, :],
        "gma": g_mix[l][None, :A_WIDTH],
        "gmb": g_mix[l][None, A_WIDTH:A_WIDTH + B_WIDTH],
        "gmc": g_mix[l][None, A_WIDTH + B_WIDTH:],
        "ws": w_spatial[l],
        "bsp_p": jnp.repeat(b_spatial[l].T, HEAD_DIM, axis=-1),
        "wst": jnp.tile(w_spatial[l][:, :ts, :ts], (1, db, db)),
        "bsp_s": jnp.tile(jnp.repeat(b_spatial[l][:, :ts].T, HEAD_DIM, axis=-1), (db, 1)),
    }
    del d
    return w


def kernel(x_prompt, x_sample, cache_k, cache_v, cache_logf, state_ret, page_table, c_prompt, c_sample,
           g_norm, w_ada, b_ada, w_ffn_gate, w_ffn_up, w_ffn_down, w_in, g_a_v, w_spatial, b_spatial,
           g_qnorm, g_knorm, b_forget, g_mix, w_out):
    b, t, d = x_prompt.shape
    db, ts, _ = x_sample.shape
    depth = w_in.shape[0]
    ns = db * ts
    n_pool, page = cache_k.shape[1], cache_k.shape[2]
    n_pages = page_table.shape[1]
    past = n_pages * page

    tm = min(512, t)
    ta = min(512, t)
    nf = 2
    gp = min(16, n_pages)

    cst = _constants(t, ts, db, past, tm)
    triu_page = jnp.triu(jnp.ones((page, page), F32)).astype(BF16)

    mod = _ada_call(jnp.concatenate([c_prompt, c_sample], axis=0), w_ada, b_ada)

    kc = jnp.transpose(cache_k, (0, 1, 3, 4, 2)).reshape(depth, n_pool, B_WIDTH, page)
    vc = jnp.transpose(cache_v, (0, 1, 3, 4, 2)).reshape(depth, n_pool, B_WIDTH, page)
    lfc_t = jnp.swapaxes(cache_logf, 2, 3)

    x_p = x_prompt.reshape(b * t, d)
    x_s = x_sample.reshape(ns, d)
    outs = {k: [] for k in ("kp", "vp", "lp", "rp", "ks", "vs", "ls", "rs", "av")}
    head_idx = jnp.arange(C_HEADS)
    for l in range(depth):
        mod_p = mod[l, :b].reshape(b, 1, -1)
        mod_s = jnp.repeat(mod[l, b:], ts, axis=0)
        w = _layer_weights(l, w_in, g_a_v, w_spatial, b_spatial, g_qnorm, g_knorm, b_forget, g_mix, ts, db)
        ffn_w = [(w_ffn_gate[l, i].astype(BF16), w_ffn_up[l, i].astype(BF16), w_ffn_down[l, i].astype(BF16))
                 for i in range(2)]
        w_out_l = w_out[l].astype(BF16)
        prompt_cfg = dict(per_row=False, tm=tm, seq_len=t, nf=nf)
        sample_cfg = dict(per_row=True, tm=ns, seq_len=ts, nf=nf)

        x_p = _ffn_call(x_p, mod_p, 0, g_norm[l, 0][None, :], *ffn_w[0], **prompt_cfg)
        x_s = _ffn_call(x_s, mod_s, 0, g_norm[l, 0][None, :], *ffn_w[0], **sample_cfg)

        (kt32, vt32, logf, qaug, kaug, vt16, oa, oc, sret) = _proj_prompt_call(
            x_p.reshape(b, t, d), mod_p, g_norm[l, 1][None, :], w, cst, tm=tm, ta=ta)
        ob = _fox_prompt_call(qaug, kaug, vt16, ta=ta, nh=FOX_HEADS_PER_STEP)
        merge_p = (oa.reshape(b * t, -1), ob.reshape(b * t, -1), oc.reshape(b * t, -1),
                   w["gmb"], cst["seg"], w_out_l)
        outs["kp"].append(jnp.transpose(kt32.reshape(b, B_HEADS, HEAD_DIM, t), (0, 3, 1, 2)))
        outs["vp"].append(jnp.transpose(vt32.reshape(b, B_HEADS, HEAD_DIM, t), (0, 3, 1, 2)))
        outs["lp"].append(logf)
        s5 = sret.reshape(b, C_HEADS, HEAD_DIM, C_HEADS, HEAD_DIM)
        outs["rp"].append(jnp.moveaxis(s5[:, head_idx, :, head_idx, :], 0, 1))

        s0_blk = jnp.einsum("bhde,hg->bhdge", state_ret[l], jnp.eye(C_HEADS, dtype=F32)).reshape(
            db * C_WIDTH, C_WIDTH)
        (k32s, v32s, logfs, q32s, avs, oas, ocs, srets) = _proj_sample_call(
            x_s, mod_s, g_norm[l, 1][None, :], w, cst, s0_blk, ts=ts, db=db)
        obs = _fox_sample_call(
            page_table, q32s.reshape(db, ts, -1), k32s.reshape(db, ts, -1), v32s.reshape(db, ts, -1),
            jnp.swapaxes(logfs.reshape(db, ts, B_HEADS), 1, 2), triu_page, kc, vc, lfc_t, l, gp=gp)
        merge_s = (oas, obs.reshape(ns, -1), ocs, w["gmb"], cst["seg"], w_out_l)
        outs["ks"].append(k32s.reshape(db, ts, B_HEADS, HEAD_DIM))
        outs["vs"].append(v32s.reshape(db, ts, B_HEADS, HEAD_DIM))
        outs["ls"].append(logfs.reshape(db, ts, B_HEADS))
        s5 = srets.reshape(db, C_HEADS, HEAD_DIM, C_HEADS, HEAD_DIM)
        outs["rs"].append(jnp.moveaxis(s5[:, head_idx, :, head_idx, :], 0, 1))
        outs["av"].append(avs.reshape(db, ts, A_WIDTH))

        x_p = _ffn_call(x_p, mod_p, 2, g_norm[l, 2][None, :], *ffn_w[1], merge=merge_p, **prompt_cfg)
        x_s = _ffn_call(x_s, mod_s, 2, g_norm[l, 2][None, :], *ffn_w[1], merge=merge_s, **sample_cfg)

    st = lambda key: jnp.stack(outs[key])
    return (x_p.reshape(b, t, d), x_s.reshape(db, ts, d), st("kp"), st("vp"), st("lp"), st("rp"),
            st("ks"), st("vs"), st("ls"), st("rs"), st("av"))
```

```python
import functools

import numpy as np
import jax
import jax.numpy as jnp
from jax import lax
from jax.experimental import pallas as pl
from jax.experimental.pallas import tpu as pltpu

F32 = jnp.float32
BF16 = jnp.bfloat16

HEAD_DIM = 64
A_HEADS, B_HEADS, C_HEADS = 4, 8, 4
A_WIDTH, B_WIDTH, C_WIDTH = A_HEADS * HEAD_DIM, B_HEADS * HEAD_DIM, C_HEADS * HEAD_DIM
CHUNK = 128
ROPE_BASE = 10000.0
EPS = 1e-6
N_SUBLAYERS = 3
LANES = 128
MXU_DIM = 256
NEG_BIG = -1e30
FOX_HEADS_PER_STEP = 4
SAMPLE_PAGE_GROUPS = 8
NPIECE = 3
V_ROWS = 80
LOG2E = 1.4426950408889634
VMEM_LIMIT = 56 * 1024 * 1024


def _dot(a, b):
    return jnp.dot(a, b, preferred_element_type=F32)


def _dot_nt(a, b):
    return lax.dot_general(a, b, (((1,), (1,)), ((), ())), preferred_element_type=F32)


def _silu(x):
    return x / (1.0 + jnp.exp(-x))


def _gelu_tanh(x):
    c = 0.7978845608028654
    return 0.5 * x * (1.0 + jnp.tanh(c * (x + 0.044715 * (x * x * x))))


def _log_sigmoid(x):
    return jnp.minimum(x, 0.0) - jnp.log1p(jnp.exp(-jnp.abs(x)))


def _modnorm(x, g, shift, scale):
    ms = jnp.mean(x * x, axis=-1, keepdims=True)
    return (x * lax.rsqrt(ms + EPS) * g) * (1.0 + scale) + shift


def _split_bf16(x, parts):
    out = []
    r = x
    for i in range(parts):
        p = r.astype(BF16)
        out.append(p)
        if i + 1 < parts:
            r = r - p.astype(F32)
    return out


def _dot_split_lhs(a, b_bf16, parts):
    acc = None
    for p in _split_bf16(a, parts):
        t = _dot(p, b_bf16)
        acc = t if acc is None else acc + t
    return acc


def _dot_split_rhs(a_bf16, b, parts):
    acc = None
    for p in _split_bf16(b, parts):
        t = _dot(a_bf16, p)
        acc = t if acc is None else acc + t
    return acc


def _head_mean_sq(x, seg):
    w = x.shape[-1]
    outs = []
    for c in range(w // MXU_DIM):
        xs = x[:, c * MXU_DIM:(c + 1) * MXU_DIM]
        outs.append(_dot_split_lhs(xs * xs, seg, 1))
    ss = outs[0] if len(outs) == 1 else jnp.concatenate(outs, axis=-1)
    return ss * (1.0 / HEAD_DIM)


def _head_norm(x, seg):
    return x * lax.rsqrt(_head_mean_sq(x, seg) + EPS)


def _rope(x, cos, sin_signed):
    outs = []
    for c in range(x.shape[-1] // LANES):
        xs = x[:, c * LANES:(c + 1) * LANES]
        fwd = pltpu.roll(xs, HEAD_DIM // 2, axis=1)
        bwd = pltpu.roll(xs, LANES - HEAD_DIM // 2, axis=1)
        lane = lax.broadcasted_iota(jnp.int32, xs.shape, 1)
        outs.append(jnp.where((lane % HEAD_DIM) < HEAD_DIM // 2, bwd, fwd))
    partner = jnp.concatenate(outs, axis=-1)
    return x * cos + partner * sin_signed


def _tile_rows_masked(x, nrep, rows_per, cols_per):
    xt = jnp.concatenate([x] * nrep, axis=0)
    r = lax.broadcasted_iota(jnp.int32, xt.shape, 0) // rows_per
    c = lax.broadcasted_iota(jnp.int32, xt.shape, 1) // cols_per
    return jnp.where(r == c, xt, 0.0)


def _tile_cols_masked(x, nrep, rows_per, cols_per):
    xt = jnp.concatenate([x] * nrep, axis=1)
    r = lax.broadcasted_iota(jnp.int32, xt.shape, 0) // rows_per
    c = lax.broadcasted_iota(jnp.int32, xt.shape, 1) // cols_per
    return jnp.where(r == c, xt, 0.0)


def _ada_kernel(c_ref, w_ref, b_ref, o_ref):
    a = _silu(c_ref[...]).astype(BF16)
    o_ref[...] = _dot(a, w_ref[...].astype(BF16)) + b_ref[...]


def _ada_call(c_all, w_ada, b_ada):
    depth, d, ncol = w_ada.shape
    rows = c_all.shape[0]
    tn = 1024
    return pl.pallas_call(
        _ada_kernel,
        grid=(depth, ncol // tn),
        in_specs=[
            pl.BlockSpec((rows, d), lambda l, n: (0, 0)),
            pl.BlockSpec((None, d, tn), lambda l, n: (l, 0, n)),
            pl.BlockSpec((None, 1, tn), lambda l, n: (l, 0, n)),
        ],
        out_specs=pl.BlockSpec((None, rows, tn), lambda l, n: (l, 0, n)),
        out_shape=jax.ShapeDtypeStruct((depth, rows, ncol), F32),
        compiler_params=pltpu.CompilerParams(
            dimension_semantics=("arbitrary", "arbitrary"), vmem_limit_bytes=VMEM_LIMIT),
        name="adaln",
    )(c_all, w_ada, b_ada.reshape(depth, 1, ncol))


def _mod_specs(per_row, tm, d, tiles_per_seq, sub, which):
    col = sub * 3 + which
    if per_row:
        return pl.BlockSpec((tm, d), lambda i: (i, col))
    return pl.BlockSpec((None, 1, d), lambda i: (i // tiles_per_seq, 0, col))


def _half_ffn(x, shift, scale, gate_mod, g, wg_ref, wu_ref, wd_ref, nf):
    d_ff = wg_ref.shape[1]
    tf = d_ff // nf
    h = _modnorm(x, g, shift, scale).astype(BF16)
    acc = None
    for f in range(nf):
        cols = slice(f * tf, (f + 1) * tf)
        gate = _dot(h, wg_ref[:, cols])
        up = _dot(h, wu_ref[:, cols])
        a = (_silu(gate) * up).astype(BF16)
        t = _dot(a, wd_ref[cols, :])
        acc = t if acc is None else acc + t
    return x + (0.5 * gate_mod) * acc


def _ffn_kernel(x_ref, sh_ref, sc_ref, gt_ref, g_ref, wg_ref, wu_ref, wd_ref, o_ref, *, nf):
    o_ref[...] = _half_ffn(x_ref[...], sh_ref[...], sc_ref[...], gt_ref[...], g_ref[...],
                           wg_ref, wu_ref, wd_ref, nf)


def _merge_ffn_kernel(oa_ref, ob_ref, oc_ref, x_ref, gt1_ref, gmb_ref, seg_ref, wo_ref,
                      sh_ref, sc_ref, gt_ref, g_ref, wg_ref, wu_ref, wd_ref, o_ref, *, nf):
    ob = (_head_norm(ob_ref[...], seg_ref[...]) * gmb_ref[...]).astype(BF16)
    ab = A_WIDTH + B_WIDTH
    y = (_dot(oa_ref[...], wo_ref[:A_WIDTH, :]) + _dot(ob, wo_ref[A_WIDTH:ab, :])
         + _dot(oc_ref[...], wo_ref[ab:, :]))
    x = x_ref[...] + gt1_ref[...] * y
    o_ref[...] = _half_ffn(x, sh_ref[...], sc_ref[...], gt_ref[...], g_ref[...], wg_ref, wu_ref, wd_ref, nf)


def _ffn_call(x2, mod, sub, g, wg, wu, wd, *, per_row, tm, seq_len, nf, merge=None):
    n, d = x2.shape
    d_ff = wg.shape[1]
    tiles_per_seq = seq_len // tm if not per_row else 1
    const2 = lambda i: (0, 0)
    rows = lambda width: pl.BlockSpec((tm, width), lambda i: (i, 0))
    resident = lambda shape: pl.BlockSpec(shape, const2, pipeline_mode=pl.Buffered(1))
    ffn_specs = [
        _mod_specs(per_row, tm, d, tiles_per_seq, sub, 0),
        _mod_specs(per_row, tm, d, tiles_per_seq, sub, 1),
        _mod_specs(per_row, tm, d, tiles_per_seq, sub, 2),
        pl.BlockSpec((1, d), const2),
        resident((d, d_ff)), resident((d, d_ff)), resident((d_ff, d)),
    ]
    ffn_args = (mod, mod, mod, g, wg, wu, wd)
    if merge is None:
        body, in_specs, args = _ffn_kernel, [rows(d)] + ffn_specs, (x2,) + ffn_args
    else:
        oa, ob, oc, gmb, seg, w_out = merge
        body = _merge_ffn_kernel
        in_specs = [rows(A_WIDTH), rows(B_WIDTH), rows(C_WIDTH), rows(d),
                    _mod_specs(per_row, tm, d, tiles_per_seq, 1, 2),
                    pl.BlockSpec((1, B_WIDTH), const2), pl.BlockSpec((MXU_DIM, MXU_DIM), const2),
                    resident((d, d))] + ffn_specs
        args = (oa, ob, oc, x2, mod, gmb, seg, w_out) + ffn_args
    return pl.pallas_call(
        functools.partial(body, nf=nf),
        grid=(n // tm,),
        in_specs=in_specs,
        out_specs=rows(d),
        out_shape=jax.ShapeDtypeStruct((n, d), F32),
        compiler_params=pltpu.CompilerParams(
            dimension_semantics=("arbitrary",), vmem_limit_bytes=VMEM_LIMIT),
        name="half_ffn" if merge is None else "merge_ffn",
    )(*args)


def _proj_prompt_kernel(
        x_ref, sh_ref, sc_ref, g_ref, wa_ref, wb_ref, wf_ref, wc_ref,
        gav_ref, gq_ref, gk_ref, bf_ref, gma_ref, gmc_ref, cos_ref, sin_ref,
        ws_ref, bsp_ref, dintra_ref, dq_ref, dk_ref, dch_ref, seg_ref, tril_ref,
        place_ref, pconst_ref, *rest, tm, ta, n_prev):
    n_in = 2 if n_prev else 0
    prev_refs = rest[:n_in]
    (kt32_ref, vt32_ref, logf_ref, qaug_ref, kaug_ref, vt16_ref,
     oa_ref, oc_ref, sret_ref, s_scr, cc_scr) = rest[n_in:]
    j = pl.program_id(1)

    @pl.when(j == 0)
    def _():
        s_scr[...] = jnp.zeros_like(s_scr)
        cc_scr[...] = jnp.zeros_like(cc_scr)

    x = x_ref[...]
    hb = _modnorm(x, g_ref[...], sh_ref[...], sc_ref[...]).astype(BF16)
    seg = seg_ref[...]

    za = _dot(hb, wa_ref[...])
    zb = _dot(hb, wb_ref[...])
    zf = _dot(hb, wf_ref[...])
    zc = _dot(hb, wc_ref[...])

    a_u = _gelu_tanh(za[:, :A_WIDTH])
    a_vr = _gelu_tanh(za[:, A_WIDTH:])
    a_v = a_vr * lax.rsqrt(jnp.mean(a_vr * a_vr, axis=-1, keepdims=True) + EPS) * gav_ref[...]

    q_b = _head_norm(zb[:, :B_WIDTH], seg) * gq_ref[...]
    k_b = _head_norm(zb[:, B_WIDTH:2 * B_WIDTH], seg) * gk_ref[...]
    v_b = zb[:, 2 * B_WIDTH:]
    for i in range(n_prev):
        kt32_ref[i] = prev_refs[0][i]
        vt32_ref[i] = prev_refs[1][i]
    kt32_ref[n_prev] = jnp.transpose(k_b)
    v_t = jnp.transpose(v_b)
    vt32_ref[n_prev] = v_t
    v_t16 = v_t.astype(BF16)
    pad_row = lax.broadcasted_iota(jnp.int32, (V_ROWS - HEAD_DIM, ta), 0)
    ones_pad = jnp.where(pad_row == 0, 1.0, 0.0).astype(BF16)
    for h in range(B_HEADS):
        for c in range(tm // ta):
            vt16_ref[h, c] = jnp.concatenate(
                [v_t16[h * HEAD_DIM:(h + 1) * HEAD_DIM, c * ta:(c + 1) * ta], ones_pad], axis=0)

    logf = _log_sigmoid(zf + bf_ref[...])
    logf_ref[...] = logf[:, :B_HEADS]
    cum_c = _dot_split_rhs(tril_ref[...], logf, 3) + cc_scr[...]
    cc_scr[...] = cum_c[tm - 1:tm, :]
    pieces = jnp.concatenate(_split_bf16(cum_c * LOG2E, NPIECE), axis=-1)
    aux = (_dot(pieces, place_ref[...]) + pconst_ref[...]).astype(BF16)
    k_aux = aux[:, :LANES]
    q_all = aux[:, LANES:]
    q16 = (q_b * LOG2E).astype(BF16)
    k16 = k_b.astype(BF16)
    lane = lax.broadcasted_iota(jnp.int32, (tm, LANES), 1)
    zero16 = jnp.zeros((tm, LANES), BF16)
    for hp in range(B_HEADS // 2):
        pair = slice(hp * LANES, (hp + 1) * LANES)
        kaug_ref[hp, :, :LANES] = k16[:, pair]
        kaug_ref[hp, :, LANES:] = k_aux
        for i in range(2):
            h = 2 * hp + i
            own = (lane < HEAD_DIM) if i == 0 else (lane >= HEAD_DIM)
            qaug_ref[h, :, :LANES] = jnp.where(own, q16[:, pair], zero16)
            rel = lane - NPIECE * h
            mine = ((rel >= 0) & (rel < NPIECE)) | ((rel >= NPIECE * B_HEADS) & (rel < NPIECE * (B_HEADS + 1)))
            qaug_ref[h, :, LANES:] = jnp.where(mine, q_all, zero16)

    cos = cos_ref[...]
    sin = sin_ref[...]
    q_c = _rope(zc[:, :C_WIDTH], cos, sin)
    k_c = _rope(zc[:, C_WIDTH:2 * C_WIDTH], cos, sin) * (HEAD_DIM ** -0.5)
    v_c = zc[:, 2 * C_WIDTH:3 * C_WIDTH]
    c_g = zc[:, 3 * C_WIDTH:]

    row = lax.broadcasted_iota(jnp.int32, (CHUNK, CHUNK), 0)
    col = lax.broadcasted_iota(jnp.int32, (CHUNK, CHUNK), 1)
    w_cat = jnp.concatenate(
        [jnp.where(col <= row, ws_ref[h], 0.0) for h in range(A_HEADS)], axis=1).astype(BF16)
    dintra = dintra_ref[...]
    dq = dq_ref[...]
    dk = dk_ref[...]
    dch = dch_ref[...]
    bsp = bsp_ref[...]
    sr = lax.broadcasted_iota(jnp.int32, (C_WIDTH, C_WIDTH), 0) // HEAD_DIM
    scol = lax.broadcasted_iota(jnp.int32, (C_WIDTH, C_WIDTH), 1) // HEAD_DIM
    s_mask = sr == scol
    s_blk = s_scr[...]
    oa_parts = []
    oc_parts = []
    for c in range(tm // CHUNK):
        sl = slice(c * CHUNK, (c + 1) * CHUNK)
        av_blk = _tile_rows_masked(a_v[sl], A_HEADS, CHUNK, HEAD_DIM).astype(BF16)
        oa_parts.append(a_u[sl] * (_dot(w_cat, av_blk) + bsp))

        qc = q_c[sl].astype(BF16)
        k_blk = _tile_rows_masked(k_c[sl], C_HEADS, CHUNK, HEAD_DIM).astype(BF16)
        v_blk = _tile_rows_masked(v_c[sl], C_HEADS, CHUNK, HEAD_DIM).astype(BF16)
        att = _dot_nt(qc, k_blk) * dintra
        intra = _dot(att.astype(BF16), v_blk)
        inter = _dot(qc, s_blk.astype(BF16)) * dq
        oc_parts.append(intra + inter)
        kd_t = jnp.transpose(k_c[sl] * dk).astype(BF16)
        upd = _dot(kd_t, v_c[sl].astype(BF16))
        s_blk = s_blk * dch + jnp.where(s_mask, upd, 0.0)
    s_scr[...] = s_blk
    sret_ref[...] = s_blk

    o_a = oa_parts[0] if len(oa_parts) == 1 else jnp.concatenate(oa_parts, axis=0)
    o_c = oc_parts[0] if len(oc_parts) == 1 else jnp.concatenate(oc_parts, axis=0)
    oa_ref[...] = (_head_norm(o_a, seg) * gma_ref[...]).astype(BF16)
    oc_ref[...] = (_head_norm(o_c, seg) * gmc_ref[...] * _silu(c_g)).astype(BF16)


def _proj_prompt_call(x3, mod, g, w, cst, kv_prev, *, tm, ta):
    b, t, d = x3.shape
    n_prev = 0 if kv_prev is None else kv_prev[0].shape[0]
    nt = t // tm
    full2 = lambda bb, j: (0, 0)
    full3 = lambda bb, j: (0, 0, 0)
    row_spec = lambda width: pl.BlockSpec((None, tm, width), lambda bb, j: (bb, j, 0))
    mod_spec = lambda col: pl.BlockSpec((None, 1, d), lambda bb, j: (bb, 0, col))
    vec = lambda width: pl.BlockSpec((1, width), full2)
    in_specs = [
        row_spec(d), mod_spec(3), mod_spec(4), vec(d),
        pl.BlockSpec((d, 2 * A_WIDTH), full2), pl.BlockSpec((d, 3 * B_WIDTH), full2),
        pl.BlockSpec((d, LANES), full2), pl.BlockSpec((d, 4 * C_WIDTH), full2),
        vec(A_WIDTH), vec(B_WIDTH), vec(B_WIDTH), vec(LANES), vec(A_WIDTH), vec(C_WIDTH),
        pl.BlockSpec((tm, C_WIDTH), lambda bb, j: (j, 0)),
        pl.BlockSpec((tm, C_WIDTH), lambda bb, j: (j, 0)),
        pl.BlockSpec((A_HEADS, CHUNK, CHUNK), full3),
        pl.BlockSpec((CHUNK, A_WIDTH), full2),
        pl.BlockSpec((CHUNK, C_HEADS * CHUNK), full2),
        pl.BlockSpec((CHUNK, C_WIDTH), full2), pl.BlockSpec((CHUNK, C_WIDTH), full2),
        vec(C_WIDTH),
        pl.BlockSpec((MXU_DIM, MXU_DIM), full2),
        pl.BlockSpec((tm, tm), full2),
        pl.BlockSpec((NPIECE * LANES, 2 * LANES), full2), vec(2 * LANES),
    ]
    nhp = B_HEADS // 2
    stack_spec = lambda n: pl.BlockSpec((n, None, B_WIDTH, tm), lambda bb, j: (0, bb, 0, j))
    prev_args = ()
    if n_prev:
        in_specs += [stack_spec(n_prev), stack_spec(n_prev)]
        prev_args = tuple(kv_prev)
    col_spec = stack_spec(n_prev + 1)
    out_shape = [
        jax.ShapeDtypeStruct((n_prev + 1, b, B_WIDTH, t), F32), jax.ShapeDtypeStruct((n_prev + 1, b, B_WIDTH, t), F32),
        jax.ShapeDtypeStruct((b, t, B_HEADS), F32),
        jax.ShapeDtypeStruct((b, B_HEADS, t, 2 * LANES), BF16),
        jax.ShapeDtypeStruct((b, nhp, t, 2 * LANES), BF16),
        jax.ShapeDtypeStruct((b, B_HEADS, t // ta, V_ROWS, ta), BF16),
        jax.ShapeDtypeStruct((b, t, A_WIDTH), BF16), jax.ShapeDtypeStruct((b, t, C_WIDTH), BF16),
        jax.ShapeDtypeStruct((b, C_WIDTH, C_WIDTH), F32),
    ]
    out_specs = [
        col_spec, col_spec, row_spec(B_HEADS),
        pl.BlockSpec((None, B_HEADS, tm, 2 * LANES), lambda bb, j: (bb, 0, j, 0)),
        pl.BlockSpec((None, nhp, tm, 2 * LANES), lambda bb, j: (bb, 0, j, 0)),
        pl.BlockSpec((None, B_HEADS, tm // ta, V_ROWS, ta), lambda bb, j: (bb, 0, j, 0, 0)),
        row_spec(A_WIDTH), row_spec(C_WIDTH),
        pl.BlockSpec((None, C_WIDTH, C_WIDTH), lambda bb, j: (bb, 0, 0)),
    ]
    return pl.pallas_call(
        functools.partial(_proj_prompt_kernel, tm=tm, ta=ta, n_prev=n_prev),
        grid=(b, nt),
        in_specs=in_specs,
        out_specs=out_specs,
        out_shape=out_shape,
        scratch_shapes=[
            pltpu.VMEM((C_WIDTH, C_WIDTH), F32),
            pltpu.VMEM((1, LANES), F32),
        ],
        compiler_params=pltpu.CompilerParams(
            dimension_semantics=("arbitrary", "arbitrary"), vmem_limit_bytes=VMEM_LIMIT),
        name="mix_proj_prompt",
    )(x3, mod, mod, g, w["wa"], w["wb"], w["wf"], w["wc"],
      w["gav"], w["gq"], w["gk"], w["bf"], w["gma"], w["gmc"], cst["cos_p"], cst["sin_p"],
      w["ws"], w["bsp_p"], cst["dintra_p"], cst["dq_p"], cst["dk_p"], cst["dch_p"],
      cst["seg"], cst["tril"], cst["place"], cst["pconst"], *prev_args)


def _fox_prompt_kernel(q_ref, k_ref, vt_ref, o_ref, s_scr, m_scr, acc_scr, *, ta, nh):
    j = pl.program_id(2)
    q_heads = [q_ref[h] for h in range(nh)]
    kpos = lax.broadcasted_iota(jnp.int32, (ta, ta), 0)
    qpos = lax.broadcasted_iota(jnp.int32, (ta, ta), 1)
    causal = kpos <= qpos

    def scores_into(slot, kb):
        start = pl.multiple_of(kb * ta, ta)
        for hp in range(nh // 2):
            kblk = k_ref[hp, pl.ds(start, ta), :]
            for h in (2 * hp, 2 * hp + 1):
                s_scr[slot, h] = _dot_nt(kblk, q_heads[h])

    def consume(slot, kb, masked):
        for h in range(nh):
            s = s_scr[slot, h]
            if masked:
                s = jnp.where(causal, s, NEG_BIG)
            m = m_scr[h]
            m_new = jnp.maximum(m, jnp.max(s, axis=0, keepdims=True))
            p = jnp.exp2(s - m_new)
            alpha = jnp.exp2(m - m_new)
            acc_scr[h] = alpha * acc_scr[h] + _dot(vt_ref[h, kb], p.astype(BF16))
            m_scr[h] = m_new

    m_scr[...] = jnp.full(m_scr.shape, NEG_BIG, F32)
    acc_scr[...] = jnp.zeros_like(acc_scr)
    scores_into(0, 0)

    def pair(i, carry):
        scores_into(1, 2 * i + 1)
        consume(0, 2 * i, False)
        scores_into(0, 2 * i + 2)
        consume(1, 2 * i + 1, False)
        return carry

    lax.fori_loop(0, j // 2, pair, 0)

    @pl.when(j % 2 == 0)
    def _():
        consume(0, j, True)

    @pl.when(j % 2 == 1)
    def _():
        scores_into(1, j)
        consume(0, j - 1, False)
        consume(1, j, True)

    o_t = jnp.concatenate([acc_scr[h, :HEAD_DIM, :] / acc_scr[h, HEAD_DIM:HEAD_DIM + 1, :] for h in range(nh)], axis=0)
    o_ref[...] = jnp.transpose(o_t)


def _fox_prompt_call(qaug, kaug, vt16, *, ta, nh):
    b, _, t, wq = qaug.shape
    return pl.pallas_call(
        functools.partial(_fox_prompt_kernel, ta=ta, nh=nh),
        grid=(b, B_HEADS // nh, t // ta),
        in_specs=[
            pl.BlockSpec((None, nh, ta, wq), lambda bb, hg, j: (bb, hg, j, 0)),
            pl.BlockSpec((None, nh // 2, t, wq), lambda bb, hg, j: (bb, hg, 0, 0)),
            pl.BlockSpec((None, nh, t // ta, V_ROWS, ta), lambda bb, hg, j: (bb, hg, 0, 0, 0)),
        ],
        out_specs=pl.BlockSpec((None, ta, nh * HEAD_DIM), lambda bb, hg, j: (bb, j, hg)),
        out_shape=jax.ShapeDtypeStruct((b, t, B_WIDTH), F32),
        scratch_shapes=[
            pltpu.VMEM((2, nh, ta, ta), F32),
            pltpu.VMEM((nh, 1, ta), F32),
            pltpu.VMEM((nh, V_ROWS, ta), F32),
        ],
        compiler_params=pltpu.CompilerParams(
            dimension_semantics=("arbitrary", "arbitrary", "arbitrary"), vmem_limit_bytes=VMEM_LIMIT),
        name="fox_prompt",
    )(qaug, kaug, vt16)


def _proj_sample_kernel(
        x_ref, sh_ref, sc_ref, g_ref, wa_ref, wb_ref, wf_ref, wc_ref,
        gav_ref, gq_ref, gk_ref, bf_ref, gma_ref, gmc_ref, cos_ref, sin_ref,
        wst_ref, bsp_ref, dintra_ref, dq_ref, dk_ref, dch_ref, seg_ref, s0_ref,
        k32_ref, v32_ref, logf_ref, q32_ref, av_ref, oa_ref, oc_ref, sret_ref, *, ns, ts, db):
    x = x_ref[...]
    hb = _modnorm(x, g_ref[...], sh_ref[...], sc_ref[...]).astype(BF16)
    seg = seg_ref[...]

    za = _dot(hb, wa_ref[...])
    a_u = _gelu_tanh(za[:, :A_WIDTH])
    a_vr = _gelu_tanh(za[:, A_WIDTH:])
    a_v = a_vr * lax.rsqrt(jnp.mean(a_vr * a_vr, axis=-1, keepdims=True) + EPS) * gav_ref[...]
    av_ref[...] = a_v

    zb = _dot(hb, wb_ref[...])
    q32_ref[...] = _head_norm(zb[:, :B_WIDTH], seg) * gq_ref[...]
    k32_ref[...] = _head_norm(zb[:, B_WIDTH:2 * B_WIDTH], seg) * gk_ref[...]
    v32_ref[...] = zb[:, 2 * B_WIDTH:]
    logf = _log_sigmoid(_dot(hb, wf_ref[...]) + bf_ref[...])
    logf_ref[...] = logf[:, :B_HEADS]

    zc = _dot(hb, wc_ref[...])
    cos = cos_ref[...]
    sin = sin_ref[...]
    q_c = _rope(zc[:, :C_WIDTH], cos, sin)
    k_c = _rope(zc[:, C_WIDTH:2 * C_WIDTH], cos, sin) * (HEAD_DIM ** -0.5)
    v_c = zc[:, 2 * C_WIDTH:3 * C_WIDTH]
    c_g = zc[:, 3 * C_WIDTH:]

    row = lax.broadcasted_iota(jnp.int32, (ns, ns), 0)
    col = lax.broadcasted_iota(jnp.int32, (ns, ns), 1)
    keep = (row // ts == col // ts) & (col <= row)
    w_cat = jnp.concatenate(
        [jnp.where(keep, wst_ref[h], 0.0) for h in range(A_HEADS)], axis=1).astype(BF16)
    av_blk = _tile_rows_masked(a_v, A_HEADS, ns, HEAD_DIM).astype(BF16)
    o_a = a_u * (_dot(w_cat, av_blk) + bsp_ref[...])

    qc = q_c.astype(BF16)
    k_blk = _tile_rows_masked(k_c, C_HEADS, ns, HEAD_DIM).astype(BF16)
    v_blk = _tile_rows_masked(v_c, C_HEADS, ns, HEAD_DIM).astype(BF16)
    att = _dot_nt(qc, k_blk) * dintra_ref[...]
    intra = _dot(att.astype(BF16), v_blk)
    s0 = s0_ref[...]
    q_exp = _tile_cols_masked(q_c, db, ts, C_WIDTH).astype(BF16)
    inter = _dot(q_exp, s0.astype(BF16)) * dq_ref[...]
    o_c = intra + inter
    kd_t = jnp.transpose(k_c * dk_ref[...])
    kd_exp = _tile_rows_masked(kd_t, db, C_WIDTH, ts).astype(BF16)
    sret_ref[...] = s0 * dch_ref[...] + _dot(kd_exp, v_c.astype(BF16))

    oa_ref[...] = (_head_norm(o_a, seg) * gma_ref[...]).astype(BF16)
    oc_ref[...] = (_head_norm(o_c, seg) * gmc_ref[...] * _silu(c_g)).astype(BF16)


def _proj_sample_call(x2, mod_rows, g, w, cst, s0_blk, *, ts, db):
    ns, d = x2.shape
    in_arrays = [
        x2, mod_rows[:, 3 * d:4 * d], mod_rows[:, 4 * d:5 * d], g,
        w["wa"], w["wb"], w["wf"], w["wc"],
        w["gav"], w["gq"], w["gk"], w["bf"], w["gma"], w["gmc"], cst["cos_s"], cst["sin_s"],
        w["wst"], w["bsp_s"], cst["dintra_s"], cst["dq_s"], cst["dk_s"], cst["dch_s"], cst["seg"], s0_blk,
    ]
    out_shape = [
        jax.ShapeDtypeStruct((ns, B_WIDTH), F32), jax.ShapeDtypeStruct((ns, B_WIDTH), F32),
        jax.ShapeDtypeStruct((ns, B_HEADS), F32), jax.ShapeDtypeStruct((ns, B_WIDTH), F32),
        jax.ShapeDtypeStruct((ns, A_WIDTH), F32),
        jax.ShapeDtypeStruct((ns, A_WIDTH), BF16), jax.ShapeDtypeStruct((ns, C_WIDTH), BF16),
        jax.ShapeDtypeStruct(s0_blk.shape, F32),
    ]
    return pl.pallas_call(
        functools.partial(_proj_sample_kernel, ns=ns, ts=ts, db=db),
        out_shape=out_shape,
        compiler_params=pltpu.CompilerParams(vmem_limit_bytes=VMEM_LIMIT),
        name="mix_proj_sample",
    )(*in_arrays)


def _fox_sample_kernel(pt_ref, q_ref, kn_ref, vn_ref, lfn_ref, triu_ref, *rest, ts, gp, page):
    kp_refs = rest[:gp]
    vp_refs = rest[gp:2 * gp]
    lf_ref = rest[2 * gp]
    o_ref = rest[2 * gp + 1]
    qb_scr, qf_scr, m_scr, l_scr, acc_scr, cr_scr = rest[2 * gp + 2:]
    pg = pl.program_id(1)
    seq = pl.program_id(0)
    nrow = ts * B_HEADS
    hmask = (lax.broadcasted_iota(jnp.int32, (B_HEADS, B_WIDTH), 1) // HEAD_DIM
             == lax.broadcasted_iota(jnp.int32, (B_HEADS, B_WIDTH), 0))

    @pl.when(pg == 0)
    def _():
        q = q_ref[...]
        for t in range(ts):
            qt = jnp.where(hmask, jnp.broadcast_to(q[t:t + 1, :], (B_HEADS, B_WIDTH)), 0.0)
            qf_scr[t * B_HEADS:(t + 1) * B_HEADS, :] = qt
            qb_scr[t * B_HEADS:(t + 1) * B_HEADS, :] = qt.astype(BF16)
        m_scr[...] = jnp.full(m_scr.shape, NEG_BIG, F32)
        l_scr[...] = jnp.zeros_like(l_scr)
        acc_scr[...] = jnp.zeros_like(acc_scr)
        cr_scr[...] = jnp.zeros_like(cr_scr)

    qb = qb_scr[...]
    m = m_scr[...]
    l = l_scr[...]
    acc = acc_scr[...]
    carry = cr_scr[:, 0:1]
    lf_all = jnp.concatenate([lf_ref[pt_ref[seq, pg * gp + g]] for g in range(gp)], axis=0)
    local = _dot_split_lhs(lf_all, triu_ref[...], 3)
    bias = []
    for g in range(gp):
        cum_t = local[g * B_HEADS:(g + 1) * B_HEADS] + carry
        carry = cum_t[:, page - 1:page]
        bias.append(jnp.concatenate([cum_t] * ts, axis=0))
    per = max(1, gp // SAMPLE_PAGE_GROUPS)
    groups = [range(i, i + per) for i in range(0, gp, per)]
    scores = []
    for grp in groups:
        k_grp = jnp.concatenate([kp_refs[g][...].astype(BF16) for g in grp], axis=1)
        scores.append(_dot(qb, k_grp) - jnp.concatenate([bias[g] for g in grp], axis=1))
    for grp, s in zip(groups, scores):
        m_new = jnp.maximum(m, jnp.max(s, axis=-1, keepdims=True))
        p = jnp.exp(s - m_new)
        alpha = jnp.exp(m - m_new)
        l = alpha * l + jnp.sum(p, axis=-1, keepdims=True)
        v_grp = jnp.concatenate([vp_refs[g][...].astype(BF16) for g in grp], axis=1)
        acc = alpha * acc + _dot_nt(p.astype(BF16), v_grp)
        m = m_new
    m_scr[...] = m
    l_scr[...] = l
    acc_scr[...] = acc
    cr_scr[...] = jnp.broadcast_to(carry, cr_scr.shape)

    @pl.when(pg == pl.num_programs(1) - 1)
    def _():
        qf = qf_scr[...]
        kn = kn_ref[...]
        vn = vn_ref[...]
        lfn = lfn_ref[...]
        tok = lax.broadcasted_iota(jnp.int32, (nrow, 1), 0) // B_HEADS
        run = carry
        s_new = []
        for jn in range(ts):
            run = run + lfn[:, jn:jn + 1]
            sj = jnp.sum(qf * kn[jn:jn + 1, :], axis=-1, keepdims=True) - jnp.concatenate([run] * ts, axis=0)
            s_new.append(jnp.where(tok >= jn, sj, NEG_BIG))
        m2 = m
        for sj in s_new:
            m2 = jnp.maximum(m2, sj)
        alpha = jnp.exp(m - m2)
        l2 = alpha * l
        acc2 = alpha * acc
        for jn in range(ts):
            pj = jnp.exp(s_new[jn] - m2)
            l2 = l2 + pj
            acc2 = acc2 + pj * vn[jn:jn + 1, :]
        o_full = acc2 / l2
        for t in range(ts):
            blk = jnp.where(hmask, o_full[t * B_HEADS:(t + 1) * B_HEADS, :], 0.0)
            o_ref[t:t + 1, :] = jnp.sum(blk, axis=0, keepdims=True)


def _fox_sample_call(page_table, q3, kn3, vn3, lfn_t, triu, kc, vc, lfc_t, layer, *, gp):
    db, ts, _ = q3.shape
    n_pages = page_table.shape[1]
    page = kc.shape[3]
    nrow = ts * B_HEADS
    seq3 = lambda shape: pl.BlockSpec((None,) + shape, lambda b, pg, pt: (b, 0, 0))

    def page_spec(shape, g):
        return pl.BlockSpec((None, None) + shape, lambda b, pg, pt, g=g: (layer, pt[b, pg * gp + g], 0, 0))

    in_specs = [seq3((ts, B_WIDTH)), seq3((ts, B_WIDTH)), seq3((ts, B_WIDTH)), seq3((B_HEADS, ts)),
                pl.BlockSpec((page, page), lambda b, pg, pt: (0, 0))]
    in_specs += [page_spec((B_WIDTH, page), g) for g in range(gp)]
    in_specs += [page_spec((B_WIDTH, page), g) for g in range(gp)]
    in_specs += [pl.BlockSpec((None,) + lfc_t.shape[1:], lambda b, pg, pt: (layer, 0, 0, 0),
                              pipeline_mode=pl.Buffered(1))]
    grid_spec = pltpu.PrefetchScalarGridSpec(
        num_scalar_prefetch=1,
        grid=(db, n_pages // gp),
        in_specs=in_specs,
        out_specs=pl.BlockSpec((None, ts, B_WIDTH), lambda b, pg, pt: (b, 0, 0)),
        scratch_shapes=[
            pltpu.VMEM((nrow, B_WIDTH), BF16), pltpu.VMEM((nrow, B_WIDTH), F32),
            pltpu.VMEM((nrow, 1), F32), pltpu.VMEM((nrow, 1), F32),
            pltpu.VMEM((nrow, B_WIDTH), F32), pltpu.VMEM((B_HEADS, LANES), F32),
        ],
    )
    return pl.pallas_call(
        functools.partial(_fox_sample_kernel, ts=ts, gp=gp, page=page),
        grid_spec=grid_spec,
        out_shape=jax.ShapeDtypeStruct((db, ts, B_WIDTH), F32),
        compiler_params=pltpu.CompilerParams(
            dimension_semantics=("arbitrary", "arbitrary"), vmem_limit_bytes=VMEM_LIMIT),
        name="fox_sample",
    )(page_table, q3, kn3, vn3, lfn_t, triu, *([kc] * gp), *([vc] * gp), lfc_t)


def _rope_tables(pos):
    half = HEAD_DIM // 2
    inv = ROPE_BASE ** (-jnp.arange(half, dtype=F32) / half)
    ang = pos[:, None] * inv[None, :]
    cos = jnp.cos(ang)
    sin = jnp.sin(ang)
    cos_h = jnp.concatenate([cos, cos], axis=-1)
    sin_h = jnp.concatenate([-sin, sin], axis=-1)
    return jnp.tile(cos_h, (1, C_HEADS)), jnp.tile(sin_h, (1, C_HEADS))


def _decay_tables(length, log_gamma):
    idx = jnp.arange(length, dtype=F32)
    rel = idx[:, None] - idx[None, :]
    d_intra = jnp.where(rel[None] >= 0, jnp.exp(jnp.maximum(rel, 0.0)[None] * log_gamma[:, None, None]), 0.0)
    d_q = jnp.exp((idx + 1.0)[:, None] * log_gamma[None, :])
    d_k = jnp.exp((length - 1.0 - idx)[:, None] * log_gamma[None, :])
    d_chunk = jnp.exp(length * log_gamma)
    return d_intra, d_q, d_k, d_chunk


def _constants(t, ts, db, past, tm):
    log_gamma = jnp.log1p(-jnp.exp2(-5.0 - jnp.arange(C_HEADS, dtype=F32)))
    cst = {}
    cst["cos_p"], cst["sin_p"] = _rope_tables(jnp.arange(t, dtype=F32))
    cos_s, sin_s = _rope_tables(past + jnp.arange(ts, dtype=F32))
    cst["cos_s"], cst["sin_s"] = jnp.tile(cos_s, (db, 1)), jnp.tile(sin_s, (db, 1))
    rep = lambda a: jnp.repeat(a, HEAD_DIM, axis=-1)

    d_intra, d_q, d_k, d_chunk = _decay_tables(CHUNK, log_gamma)
    cst["dintra_p"] = jnp.concatenate([d_intra[h] for h in range(C_HEADS)], axis=1)
    cst["dq_p"], cst["dk_p"], cst["dch_p"] = rep(d_q), rep(d_k), rep(d_chunk[None, :])

    d_intra, d_q, d_k, d_chunk = _decay_tables(ts, log_gamma)
    same_seq = jnp.kron(jnp.eye(db, dtype=F32), jnp.ones((ts, ts), F32))
    cst["dintra_s"] = jnp.concatenate(
        [jnp.tile(d_intra[h], (db, db)) * same_seq for h in range(C_HEADS)], axis=1)
    cst["dq_s"] = jnp.tile(rep(d_q), (db, 1))
    cst["dk_s"] = jnp.tile(rep(d_k), (db, 1))
    cst["dch_s"] = rep(d_chunk[None, :])

    seg = jnp.kron(jnp.eye(MXU_DIM // HEAD_DIM, dtype=F32), jnp.ones((HEAD_DIM, HEAD_DIM), F32))
    cst["seg"] = seg.astype(BF16)
    cst["tril"] = jnp.tril(jnp.ones((tm, tm), F32)).astype(BF16)

    nb = NPIECE * B_HEADS
    place = np.zeros((NPIECE * LANES, 2 * LANES), np.float32)
    pconst = np.zeros((1, 2 * LANES), np.float32)
    pconst[0, :nb] = 1.0
    pconst[0, LANES + nb:LANES + 2 * nb] = -1.0
    for h in range(B_HEADS):
        for a in range(NPIECE):
            place[a * LANES + h, nb + NPIECE * h + a] = 1.0
            place[a * LANES + h, LANES + NPIECE * h + a] = 1.0
    cst["place"], cst["pconst"] = jnp.asarray(place, BF16), jnp.asarray(pconst)
    return cst


def _layer_weights(l, w_in, g_a_v, w_spatial, b_spatial, g_qnorm, g_knorm, b_forget, g_mix, ts, db):
    d = w_in.shape[1]
    wl = w_in[l]
    o = 0
    wa = wl[:, o:o + 2 * A_WIDTH]; o += 2 * A_WIDTH
    wb = wl[:, o:o + 3 * B_WIDTH]; o += 3 * B_WIDTH
    wf = wl[:, o:o + B_HEADS]; o += B_HEADS
    wc = wl[:, o:o + 4 * C_WIDTH]
    w = {
        "wa": wa.astype(BF16), "wb": wb.astype(BF16), "wc": wc.astype(BF16),
        "wf": jnp.pad(wf, ((0, 0), (0, LANES - B_HEADS))).astype(BF16),
        "gav": g_a_v[l][None, :],
        "gq": jnp.tile(g_qnorm[l], B_HEADS)[None, :] * (HEAD_DIM ** -0.5),
        "gk": jnp.tile(g_knorm[l], B_HEADS)[None, :],
        "bf": jnp.pad(b_forget[l], (0, LANES - B_HEADS))[None, :],
        "gma": g_mix[l][None, :A_WIDTH],
        "gmb": g_mix[l][None, A_WIDTH:A_WIDTH + B_WIDTH],
        "gmc": g_mix[l][None, A_WIDTH + B_WIDTH:],
        "ws": w_spatial[l],
        "bsp_p": jnp.repeat(b_spatial[l].T, HEAD_DIM, axis=-1),
        "wst": jnp.tile(w_spatial[l][:, :ts, :ts], (1, db, db)),
        "bsp_s": jnp.tile(jnp.repeat(b_spatial[l][:, :ts].T, HEAD_DIM, axis=-1), (db, 1)),
    }
    del d
    return w


def kernel(x_prompt, x_sample, cache_k, cache_v, cache_logf, state_ret, page_table, c_prompt, c_sample,
           g_norm, w_ada, b_ada, w_ffn_gate, w_ffn_up, w_ffn_down, w_in, g_a_v, w_spatial, b_spatial,
           g_qnorm, g_knorm, b_forget, g_mix, w_out):
    b, t, d = x_prompt.shape
    db, ts, _ = x_sample.shape
    depth = w_in.shape[0]
    ns = db * ts
    n_pool, page = cache_k.shape[1], cache_k.shape[2]
    n_pages = page_table.shape[1]
    past = n_pages * page

    tm = min(512, t)
    ta = min(512, t)
    nf = 2
    gp = min(16, n_pages)

    cst = _constants(t, ts, db, past, tm)
    triu_page = jnp.triu(jnp.ones((page, page), F32)).astype(BF16)

    mod = _ada_call(jnp.concatenate([c_prompt, c_sample], axis=0), w_ada, b_ada)

    kc = jnp.transpose(cache_k, (0, 1, 3, 4, 2)).reshape(depth, n_pool, B_WIDTH, page)
    vc = jnp.transpose(cache_v, (0, 1, 3, 4, 2)).reshape(depth, n_pool, B_WIDTH, page)
    lfc_t = jnp.swapaxes(cache_logf, 2, 3)

    x_p = x_prompt.reshape(b * t, d)
    x_s = x_sample.reshape(ns, d)
    outs = {k: [] for k in ("lp", "rp", "ks", "vs", "ls", "rs", "av")}
    kv_prev = None
    head_idx = jnp.arange(C_HEADS)
    for l in range(depth):
        mod_p = mod[l, :b].reshape(b, 1, -1)
        mod_s = jnp.repeat(mod[l, b:], ts, axis=0)
        w = _layer_weights(l, w_in, g_a_v, w_spatial, b_spatial, g_qnorm, g_knorm, b_forget, g_mix, ts, db)
        ffn_w = [(w_ffn_gate[l, i].astype(BF16), w_ffn_up[l, i].astype(BF16), w_ffn_down[l, i].astype(BF16))
                 for i in range(2)]
        w_out_l = w_out[l].astype(BF16)
        prompt_cfg = dict(per_row=False, tm=tm, seq_len=t, nf=nf)
        sample_cfg = dict(per_row=True, tm=ns, seq_len=ts, nf=nf)

        x_p = _ffn_call(x_p, mod_p, 0, g_norm[l, 0][None, :], *ffn_w[0], **prompt_cfg)
        x_s = _ffn_call(x_s, mod_s, 0, g_norm[l, 0][None, :], *ffn_w[0], **sample_cfg)

        (kt32, vt32, logf, qaug, kaug, vt16, oa, oc, sret) = _proj_prompt_call(
            x_p.reshape(b, t, d), mod_p, g_norm[l, 1][None, :], w, cst, kv_prev, tm=tm, ta=ta)
        kv_prev = (kt32, vt32)
        ob = _fox_prompt_call(qaug, kaug, vt16, ta=ta, nh=FOX_HEADS_PER_STEP)
        merge_p = (oa.reshape(b * t, -1), ob.reshape(b * t, -1), oc.reshape(b * t, -1),
                   w["gmb"], cst["seg"], w_out_l)
        outs["lp"].append(logf)
        s5 = sret.reshape(b, C_HEADS, HEAD_DIM, C_HEADS, HEAD_DIM)
        outs["rp"].append(jnp.moveaxis(s5[:, head_idx, :, head_idx, :], 0, 1))

        s0_blk = jnp.einsum("bhde,hg->bhdge", state_ret[l], jnp.eye(C_HEADS, dtype=F32)).reshape(
            db * C_WIDTH, C_WIDTH)
        (k32s, v32s, logfs, q32s, avs, oas, ocs, srets) = _proj_sample_call(
            x_s, mod_s, g_norm[l, 1][None, :], w, cst, s0_blk, ts=ts, db=db)
        obs = _fox_sample_call(
            page_table, q32s.reshape(db, ts, -1), k32s.reshape(db, ts, -1), v32s.reshape(db, ts, -1),
            jnp.swapaxes(logfs.reshape(db, ts, B_HEADS), 1, 2), triu_page, kc, vc, lfc_t, l, gp=gp)
        merge_s = (oas, obs.reshape(ns, -1), ocs, w["gmb"], cst["seg"], w_out_l)
        outs["ks"].append(k32s.reshape(db, ts, B_HEADS, HEAD_DIM))
        outs["vs"].append(v32s.reshape(db, ts, B_HEADS, HEAD_DIM))
        outs["ls"].append(logfs.reshape(db, ts, B_HEADS))
        s5 = srets.reshape(db, C_HEADS, HEAD_DIM, C_HEADS, HEAD_DIM)
        outs["rs"].append(jnp.moveaxis(s5[:, head_idx, :, head_idx, :], 0, 1))
        outs["av"].append(avs.reshape(db, ts, A_WIDTH))

        x_p = _ffn_call(x_p, mod_p, 2, g_norm[l, 2][None, :], *ffn_w[1], merge=merge_p, **prompt_cfg)
        x_s = _ffn_call(x_s, mod_s, 2, g_norm[l, 2][None, :], *ffn_w[1], merge=merge_s, **sample_cfg)

    st = lambda key: jnp.stack(outs[key])
    kv_out = lambda a: jnp.transpose(a.reshape(depth, b, B_HEADS, HEAD_DIM, t), (0, 1, 4, 2, 3))
    return (x_p.reshape(b, t, d), x_s.reshape(db, ts, d), kv_out(kv_prev[0]), kv_out(kv_prev[1]), st("lp"), st("rp"),
            st("ks"), st("vs"), st("ls"), st("rs"), st("av"))
```

```python
import functools

import numpy as np
import jax
import jax.numpy as jnp
from jax import lax
from jax.experimental import pallas as pl
from jax.experimental.pallas import tpu as pltpu

F32 = jnp.float32
BF16 = jnp.bfloat16

HEAD_DIM = 64
A_HEADS, B_HEADS, C_HEADS = 4, 8, 4
A_WIDTH, B_WIDTH, C_WIDTH = A_HEADS * HEAD_DIM, B_HEADS * HEAD_DIM, C_HEADS * HEAD_DIM
CHUNK = 128
ROPE_BASE = 10000.0
EPS = 1e-6
N_SUBLAYERS = 3
LANES = 128
MXU_DIM = 256
NEG_BIG = -1e30
FOX_HEADS_PER_STEP = 4
SAMPLE_PAGE_GROUPS = 8
NPIECE = 3
V_ROWS = 80
LOG2E = 1.4426950408889634
VMEM_LIMIT = 56 * 1024 * 1024


def _dot(a, b):
    return jnp.dot(a, b, preferred_element_type=F32)


def _dot_nt(a, b):
    return lax.dot_general(a, b, (((1,), (1,)), ((), ())), preferred_element_type=F32)


def _silu(x):
    return x / (1.0 + jnp.exp(-x))


def _gelu_tanh(x):
    c = 0.7978845608028654
    return 0.5 * x * (1.0 + jnp.tanh(c * (x + 0.044715 * (x * x * x))))


def _log_sigmoid(x):
    return jnp.minimum(x, 0.0) - jnp.log1p(jnp.exp(-jnp.abs(x)))


def _modnorm(x, g, shift, scale):
    ms = jnp.mean(x * x, axis=-1, keepdims=True)
    return (x * lax.rsqrt(ms + EPS) * g) * (1.0 + scale) + shift


def _split_bf16(x, parts):
    out = []
    r = x
    for i in range(parts):
        p = r.astype(BF16)
        out.append(p)
        if i + 1 < parts:
            r = r - p.astype(F32)
    return out


def _dot_split_lhs(a, b_bf16, parts):
    acc = None
    for p in _split_bf16(a, parts):
        t = _dot(p, b_bf16)
        acc = t if acc is None else acc + t
    return acc


def _dot_split_rhs(a_bf16, b, parts):
    acc = None
    for p in _split_bf16(b, parts):
        t = _dot(a_bf16, p)
        acc = t if acc is None else acc + t
    return acc


def _head_mean_sq(x, seg):
    w = x.shape[-1]
    outs = []
    for c in range(w // MXU_DIM):
        xs = x[:, c * MXU_DIM:(c + 1) * MXU_DIM]
        outs.append(_dot_split_lhs(xs * xs, seg, 1))
    ss = outs[0] if len(outs) == 1 else jnp.concatenate(outs, axis=-1)
    return ss * (1.0 / HEAD_DIM)


def _head_norm(x, seg):
    return x * lax.rsqrt(_head_mean_sq(x, seg) + EPS)


def _rope(x, cos, sin_signed):
    outs = []
    for c in range(x.shape[-1] // LANES):
        xs = x[:, c * LANES:(c + 1) * LANES]
        fwd = pltpu.roll(xs, HEAD_DIM // 2, axis=1)
        bwd = pltpu.roll(xs, LANES - HEAD_DIM // 2, axis=1)
        lane = lax.broadcasted_iota(jnp.int32, xs.shape, 1)
        outs.append(jnp.where((lane % HEAD_DIM) < HEAD_DIM // 2, bwd, fwd))
    partner = jnp.concatenate(outs, axis=-1)
    return x * cos + partner * sin_signed


def _tile_rows_masked(x, nrep, rows_per, cols_per):
    xt = jnp.concatenate([x] * nrep, axis=0)
    r = lax.broadcasted_iota(jnp.int32, xt.shape, 0) // rows_per
    c = lax.broadcasted_iota(jnp.int32, xt.shape, 1) // cols_per
    return jnp.where(r == c, xt, 0.0)


def _tile_cols_masked(x, nrep, rows_per, cols_per):
    xt = jnp.concatenate([x] * nrep, axis=1)
    r = lax.broadcasted_iota(jnp.int32, xt.shape, 0) // rows_per
    c = lax.broadcasted_iota(jnp.int32, xt.shape, 1) // cols_per
    return jnp.where(r == c, xt, 0.0)


def _ada_kernel(c_ref, w_ref, b_ref, o_ref):
    a = _silu(c_ref[...]).astype(BF16)
    o_ref[...] = _dot(a, w_ref[...].astype(BF16)) + b_ref[...]


def _ada_call(c_all, w_ada, b_ada):
    depth, d, ncol = w_ada.shape
    rows = c_all.shape[0]
    tn = 1024
    return pl.pallas_call(
        _ada_kernel,
        grid=(depth, ncol // tn),
        in_specs=[
            pl.BlockSpec((rows, d), lambda l, n: (0, 0)),
            pl.BlockSpec((None, d, tn), lambda l, n: (l, 0, n)),
            pl.BlockSpec((None, 1, tn), lambda l, n: (l, 0, n)),
        ],
        out_specs=pl.BlockSpec((None, rows, tn), lambda l, n: (l, 0, n)),
        out_shape=jax.ShapeDtypeStruct((depth, rows, ncol), F32),
        compiler_params=pltpu.CompilerParams(
            dimension_semantics=("arbitrary", "arbitrary"), vmem_limit_bytes=VMEM_LIMIT),
        name="adaln",
    )(c_all, w_ada, b_ada.reshape(depth, 1, ncol))


def _mod_specs(per_row, tm, d, tiles_per_seq, sub, which):
    col = sub * 3 + which
    if per_row:
        return pl.BlockSpec((tm, d), lambda i: (i, col))
    return pl.BlockSpec((None, 1, d), lambda i: (i // tiles_per_seq, 0, col))


def _half_ffn(x, shift, scale, gate_mod, g, wg_ref, wu_ref, wd_ref, nf):
    d_ff = wg_ref.shape[1]
    tf = d_ff // nf
    h = _modnorm(x, g, shift, scale).astype(BF16)
    acc = None
    for f in range(nf):
        cols = slice(f * tf, (f + 1) * tf)
        gate = _dot(h, wg_ref[:, cols])
        up = _dot(h, wu_ref[:, cols])
        a = (_silu(gate) * up).astype(BF16)
        t = _dot(a, wd_ref[cols, :])
        acc = t if acc is None else acc + t
    return x + (0.5 * gate_mod) * acc


def _ffn_kernel(x_ref, sh_ref, sc_ref, gt_ref, g_ref, wg_ref, wu_ref, wd_ref, o_ref, *, nf):
    o_ref[...] = _half_ffn(x_ref[...], sh_ref[...], sc_ref[...], gt_ref[...], g_ref[...],
                           wg_ref, wu_ref, wd_ref, nf)


def _merge_ffn_kernel(oa_ref, ob_ref, oc_ref, x_ref, gt1_ref, gmb_ref, seg_ref, wo_ref,
                      sh_ref, sc_ref, gt_ref, g_ref, wg_ref, wu_ref, wd_ref, o_ref, *, nf):
    ob = (_head_norm(ob_ref[...], seg_ref[...]) * gmb_ref[...]).astype(BF16)
    ab = A_WIDTH + B_WIDTH
    y = (_dot(oa_ref[...], wo_ref[:A_WIDTH, :]) + _dot(ob, wo_ref[A_WIDTH:ab, :])
         + _dot(oc_ref[...], wo_ref[ab:, :]))
    x = x_ref[...] + gt1_ref[...] * y
    o_ref[...] = _half_ffn(x, sh_ref[...], sc_ref[...], gt_ref[...], g_ref[...], wg_ref, wu_ref, wd_ref, nf)


def _ffn_call(x2, mod, sub, g, wg, wu, wd, layer, which, *, per_row, tm, seq_len, nf, merge=None):
    n, d = x2.shape
    d_ff = wg.shape[-1]
    tiles_per_seq = seq_len // tm if not per_row else 1
    const2 = lambda i: (0, 0)
    rows = lambda width: pl.BlockSpec((tm, width), lambda i: (i, 0))
    ffn_w = lambda shape: pl.BlockSpec((None, None) + shape, lambda i: (layer, which, 0, 0),
                                       pipeline_mode=pl.Buffered(1))
    ffn_specs = [
        _mod_specs(per_row, tm, d, tiles_per_seq, sub, 0),
        _mod_specs(per_row, tm, d, tiles_per_seq, sub, 1),
        _mod_specs(per_row, tm, d, tiles_per_seq, sub, 2),
        pl.BlockSpec((1, d), const2),
        ffn_w((d, d_ff)), ffn_w((d, d_ff)), ffn_w((d_ff, d)),
    ]
    ffn_args = (mod, mod, mod, g, wg, wu, wd)
    if merge is None:
        body, in_specs, args = _ffn_kernel, [rows(d)] + ffn_specs, (x2,) + ffn_args
    else:
        oa, ob, oc, gmb, seg, w_out = merge
        body = _merge_ffn_kernel
        in_specs = [rows(A_WIDTH), rows(B_WIDTH), rows(C_WIDTH), rows(d),
                    _mod_specs(per_row, tm, d, tiles_per_seq, 1, 2),
                    pl.BlockSpec((1, B_WIDTH), const2), pl.BlockSpec((MXU_DIM, MXU_DIM), const2),
                    pl.BlockSpec((None, d, d), lambda i: (layer, 0, 0), pipeline_mode=pl.Buffered(1))] + ffn_specs
        args = (oa, ob, oc, x2, mod, gmb, seg, w_out) + ffn_args
    return pl.pallas_call(
        functools.partial(body, nf=nf),
        grid=(n // tm,),
        in_specs=in_specs,
        out_specs=rows(d),
        out_shape=jax.ShapeDtypeStruct((n, d), F32),
        compiler_params=pltpu.CompilerParams(
            dimension_semantics=("arbitrary",), vmem_limit_bytes=VMEM_LIMIT),
        name="half_ffn" if merge is None else "merge_ffn",
    )(*args)


def _proj_prompt_kernel(
        x_ref, sh_ref, sc_ref, g_ref, wa_ref, wb_ref, wf_ref, wc_ref,
        gav_ref, gq_ref, gk_ref, bf_ref, gma_ref, gmc_ref, cos_ref, sin_ref,
        ws_ref, bsp_ref, dintra_ref, dq_ref, dk_ref, dch_ref, seg_ref, tril_ref,
        place_ref, pconst_ref, *rest, tm, ta, n_prev):
    n_in = 2 if n_prev else 0
    prev_refs = rest[:n_in]
    (kt32_ref, vt32_ref, logf_ref, qaug_ref, kaug_ref, vt16_ref,
     oa_ref, oc_ref, sret_ref, s_scr, cc_scr) = rest[n_in:]
    j = pl.program_id(1)

    @pl.when(j == 0)
    def _():
        s_scr[...] = jnp.zeros_like(s_scr)
        cc_scr[...] = jnp.zeros_like(cc_scr)

    x = x_ref[...]
    hb = _modnorm(x, g_ref[...], sh_ref[...], sc_ref[...]).astype(BF16)
    seg = seg_ref[...]

    za = _dot(hb, wa_ref[...])
    zb = _dot(hb, wb_ref[...])
    zf = _dot(hb, wf_ref[...])
    zc = _dot(hb, wc_ref[...])

    a_u = _gelu_tanh(za[:, :A_WIDTH])
    a_vr = _gelu_tanh(za[:, A_WIDTH:])
    a_v = a_vr * lax.rsqrt(jnp.mean(a_vr * a_vr, axis=-1, keepdims=True) + EPS) * gav_ref[...]

    q_b = _head_norm(zb[:, :B_WIDTH], seg) * gq_ref[...]
    k_b = _head_norm(zb[:, B_WIDTH:2 * B_WIDTH], seg) * gk_ref[...]
    v_b = zb[:, 2 * B_WIDTH:]
    for i in range(n_prev):
        kt32_ref[i] = prev_refs[0][i]
        vt32_ref[i] = prev_refs[1][i]
    kt32_ref[n_prev] = jnp.transpose(k_b)
    v_t = jnp.transpose(v_b)
    vt32_ref[n_prev] = v_t
    v_t16 = v_t.astype(BF16)
    pad_row = lax.broadcasted_iota(jnp.int32, (V_ROWS - HEAD_DIM, ta), 0)
    ones_pad = jnp.where(pad_row == 0, 1.0, 0.0).astype(BF16)
    for h in range(B_HEADS):
        for c in range(tm // ta):
            vt16_ref[h, c] = jnp.concatenate(
                [v_t16[h * HEAD_DIM:(h + 1) * HEAD_DIM, c * ta:(c + 1) * ta], ones_pad], axis=0)

    logf = _log_sigmoid(zf + bf_ref[...])
    logf_ref[...] = logf[:, :B_HEADS]
    cum_c = _dot_split_rhs(tril_ref[...], logf, 3) + cc_scr[...]
    cc_scr[...] = cum_c[tm - 1:tm, :]
    pieces = jnp.concatenate(_split_bf16(cum_c * LOG2E, NPIECE), axis=-1)
    aux = (_dot(pieces, place_ref[...]) + pconst_ref[...]).astype(BF16)
    k_aux = aux[:, :LANES]
    q_all = aux[:, LANES:]
    q16 = (q_b * LOG2E).astype(BF16)
    k16 = k_b.astype(BF16)
    lane = lax.broadcasted_iota(jnp.int32, (tm, LANES), 1)
    zero16 = jnp.zeros((tm, LANES), BF16)
    for hp in range(B_HEADS // 2):
        pair = slice(hp * LANES, (hp + 1) * LANES)
        kaug_ref[hp, :, :LANES] = k16[:, pair]
        kaug_ref[hp, :, LANES:] = k_aux
        for i in range(2):
            h = 2 * hp + i
            own = (lane < HEAD_DIM) if i == 0 else (lane >= HEAD_DIM)
            qaug_ref[h, :, :LANES] = jnp.where(own, q16[:, pair], zero16)
            rel = lane - NPIECE * h
            mine = ((rel >= 0) & (rel < NPIECE)) | ((rel >= NPIECE * B_HEADS) & (rel < NPIECE * (B_HEADS + 1)))
            qaug_ref[h, :, LANES:] = jnp.where(mine, q_all, zero16)

    cos = cos_ref[...]
    sin = sin_ref[...]
    q_c = _rope(zc[:, :C_WIDTH], cos, sin)
    k_c = _rope(zc[:, C_WIDTH:2 * C_WIDTH], cos, sin) * (HEAD_DIM ** -0.5)
    v_c = zc[:, 2 * C_WIDTH:3 * C_WIDTH]
    c_g = zc[:, 3 * C_WIDTH:]

    row = lax.broadcasted_iota(jnp.int32, (CHUNK, CHUNK), 0)
    col = lax.broadcasted_iota(jnp.int32, (CHUNK, CHUNK), 1)
    w_cat = jnp.concatenate(
        [jnp.where(col <= row, ws_ref[h], 0.0) for h in range(A_HEADS)], axis=1).astype(BF16)
    dintra = dintra_ref[...]
    dq = dq_ref[...]
    dk = dk_ref[...]
    dch = dch_ref[...]
    bsp = bsp_ref[...]
    sr = lax.broadcasted_iota(jnp.int32, (C_WIDTH, C_WIDTH), 0) // HEAD_DIM
    scol = lax.broadcasted_iota(jnp.int32, (C_WIDTH, C_WIDTH), 1) // HEAD_DIM
    s_mask = sr == scol
    s_blk = s_scr[...]
    oa_parts = []
    oc_parts = []
    for c in range(tm // CHUNK):
        sl = slice(c * CHUNK, (c + 1) * CHUNK)
        av_blk = _tile_rows_masked(a_v[sl], A_HEADS, CHUNK, HEAD_DIM).astype(BF16)
        oa_parts.append(a_u[sl] * (_dot(w_cat, av_blk) + bsp))

        qc = q_c[sl].astype(BF16)
        k_blk = _tile_rows_masked(k_c[sl], C_HEADS, CHUNK, HEAD_DIM).astype(BF16)
        v_blk = _tile_rows_masked(v_c[sl], C_HEADS, CHUNK, HEAD_DIM).astype(BF16)
        att = _dot_nt(qc, k_blk) * dintra
        intra = _dot(att.astype(BF16), v_blk)
        inter = _dot(qc, s_blk.astype(BF16)) * dq
        oc_parts.append(intra + inter)
        kd_t = jnp.transpose(k_c[sl] * dk).astype(BF16)
        upd = _dot(kd_t, v_c[sl].astype(BF16))
        s_blk = s_blk * dch + jnp.where(s_mask, upd, 0.0)
    s_scr[...] = s_blk
    sret_ref[...] = s_blk

    o_a = oa_parts[0] if len(oa_parts) == 1 else jnp.concatenate(oa_parts, axis=0)
    o_c = oc_parts[0] if len(oc_parts) == 1 else jnp.concatenate(oc_parts, axis=0)
    oa_ref[...] = (_head_norm(o_a, seg) * gma_ref[...]).astype(BF16)
    oc_ref[...] = (_head_norm(o_c, seg) * gmc_ref[...] * _silu(c_g)).astype(BF16)


def _proj_prompt_call(x3, mod, g, w, cst, kv_prev, *, tm, ta):
    b, t, d = x3.shape
    n_prev = 0 if kv_prev is None else kv_prev[0].shape[0]
    nt = t // tm
    full2 = lambda bb, j: (0, 0)
    full3 = lambda bb, j: (0, 0, 0)
    row_spec = lambda width: pl.BlockSpec((None, tm, width), lambda bb, j: (bb, j, 0))
    mod_spec = lambda col: pl.BlockSpec((None, 1, d), lambda bb, j: (bb, 0, col))
    vec = lambda width: pl.BlockSpec((1, width), full2)
    in_specs = [
        row_spec(d), mod_spec(3), mod_spec(4), vec(d),
        pl.BlockSpec((d, 2 * A_WIDTH), full2), pl.BlockSpec((d, 3 * B_WIDTH), full2),
        pl.BlockSpec((d, LANES), full2), pl.BlockSpec((d, 4 * C_WIDTH), full2),
        vec(A_WIDTH), vec(B_WIDTH), vec(B_WIDTH), vec(LANES), vec(A_WIDTH), vec(C_WIDTH),
        pl.BlockSpec((tm, C_WIDTH), lambda bb, j: (j, 0)),
        pl.BlockSpec((tm, C_WIDTH), lambda bb, j: (j, 0)),
        pl.BlockSpec((A_HEADS, CHUNK, CHUNK), full3),
        pl.BlockSpec((CHUNK, A_WIDTH), full2),
        pl.BlockSpec((CHUNK, C_HEADS * CHUNK), full2),
        pl.BlockSpec((CHUNK, C_WIDTH), full2), pl.BlockSpec((CHUNK, C_WIDTH), full2),
        vec(C_WIDTH),
        pl.BlockSpec((MXU_DIM, MXU_DIM), full2),
        pl.BlockSpec((tm, tm), full2),
        pl.BlockSpec((NPIECE * LANES, 2 * LANES), full2), vec(2 * LANES),
    ]
    nhp = B_HEADS // 2
    stack_spec = lambda n: pl.BlockSpec((n, None, B_WIDTH, tm), lambda bb, j: (0, bb, 0, j))
    prev_args = ()
    if n_prev:
        in_specs += [stack_spec(n_prev), stack_spec(n_prev)]
        prev_args = tuple(kv_prev)
    col_spec = stack_spec(n_prev + 1)
    out_shape = [
        jax.ShapeDtypeStruct((n_prev + 1, b, B_WIDTH, t), F32), jax.ShapeDtypeStruct((n_prev + 1, b, B_WIDTH, t), F32),
        jax.ShapeDtypeStruct((b, t, B_HEADS), F32),
        jax.ShapeDtypeStruct((b, B_HEADS, t, 2 * LANES), BF16),
        jax.ShapeDtypeStruct((b, nhp, t, 2 * LANES), BF16),
        jax.ShapeDtypeStruct((b, B_HEADS, t // ta, V_ROWS, ta), BF16),
        jax.ShapeDtypeStruct((b, t, A_WIDTH), BF16), jax.ShapeDtypeStruct((b, t, C_WIDTH), BF16),
        jax.ShapeDtypeStruct((b, C_WIDTH, C_WIDTH), F32),
    ]
    out_specs = [
        col_spec, col_spec, row_spec(B_HEADS),
        pl.BlockSpec((None, B_HEADS, tm, 2 * LANES), lambda bb, j: (bb, 0, j, 0)),
        pl.BlockSpec((None, nhp, tm, 2 * LANES), lambda bb, j: (bb, 0, j, 0)),
        pl.BlockSpec((None, B_HEADS, tm // ta, V_ROWS, ta), lambda bb, j: (bb, 0, j, 0, 0)),
        row_spec(A_WIDTH), row_spec(C_WIDTH),
        pl.BlockSpec((None, C_WIDTH, C_WIDTH), lambda bb, j: (bb, 0, 0)),
    ]
    return pl.pallas_call(
        functools.partial(_proj_prompt_kernel, tm=tm, ta=ta, n_prev=n_prev),
        grid=(b, nt),
        in_specs=in_specs,
        out_specs=out_specs,
        out_shape=out_shape,
        scratch_shapes=[
            pltpu.VMEM((C_WIDTH, C_WIDTH), F32),
            pltpu.VMEM((1, LANES), F32),
        ],
        compiler_params=pltpu.CompilerParams(
            dimension_semantics=("arbitrary", "arbitrary"), vmem_limit_bytes=VMEM_LIMIT),
        name="mix_proj_prompt",
    )(x3, mod, mod, g, w["wa"], w["wb"], w["wf"], w["wc"],
      w["gav"], w["gq"], w["gk"], w["bf"], w["gma"], w["gmc"], cst["cos_p"], cst["sin_p"],
      w["ws"], w["bsp_p"], cst["dintra_p"], cst["dq_p"], cst["dk_p"], cst["dch_p"],
      cst["seg"], cst["tril"], cst["place"], cst["pconst"], *prev_args)


def _fox_prompt_kernel(q_ref, qn_ref, k_ref, vt_ref, o_ref, s_scr, m_scr, acc_scr, *, ta, nh):
    j = pl.program_id(2)
    kpos = lax.broadcasted_iota(jnp.int32, (ta, ta), 0)
    qpos = lax.broadcasted_iota(jnp.int32, (ta, ta), 1)
    causal = kpos <= qpos

    def scores_into(slot, kb, queries=q_ref):
        start = pl.multiple_of(kb * ta, ta)
        for hp in range(nh // 2):
            kblk = k_ref[hp, pl.ds(start, ta), :]
            for h in (2 * hp, 2 * hp + 1):
                s_scr[slot, h] = _dot_nt(kblk, queries[h])

    def prefetch_next_tile():
        scores_into(2, 0, qn_ref)

    def consume(slot, kb, masked):
        for h in range(nh):
            s = s_scr[slot, h]
            if masked:
                s = jnp.where(causal, s, NEG_BIG)
            m = m_scr[h]
            m_new = jnp.maximum(m, jnp.max(s, axis=0, keepdims=True))
            p = jnp.exp2(s - m_new)
            alpha = jnp.exp2(m - m_new)
            acc_scr[h] = alpha * acc_scr[h] + _dot(vt_ref[h, kb], p.astype(BF16))
            m_scr[h] = m_new

    m_scr[...] = jnp.full(m_scr.shape, NEG_BIG, F32)
    acc_scr[...] = jnp.zeros_like(acc_scr)

    @pl.when(j == 0)
    def _():
        scores_into(2, 0)
        consume(2, 0, True)
        prefetch_next_tile()

    def pair(i, carry):
        scores_into(1, 2 * i + 2)
        consume(0, 2 * i + 1, False)
        scores_into(0, 2 * i + 3)
        consume(1, 2 * i + 2, False)
        return carry

    @pl.when(j > 0)
    def _():
        scores_into(0, 1)
        consume(2, 0, False)
        lax.fori_loop(0, (j - 1) // 2, pair, 0)

        @pl.when(j % 2 == 1)
        def _():
            prefetch_next_tile()
            consume(0, j, True)

        @pl.when(j % 2 == 0)
        def _():
            scores_into(1, j)
            consume(0, j - 1, False)
            prefetch_next_tile()
            consume(1, j, True)

    o_t = jnp.concatenate([acc_scr[h, :HEAD_DIM, :] / acc_scr[h, HEAD_DIM:HEAD_DIM + 1, :] for h in range(nh)], axis=0)
    o_ref[...] = jnp.transpose(o_t)


def _fox_prompt_call(qaug, kaug, vt16, *, ta, nh):
    b, _, t, wq = qaug.shape
    nt = t // ta
    return pl.pallas_call(
        functools.partial(_fox_prompt_kernel, ta=ta, nh=nh),
        grid=(b, B_HEADS // nh, nt),
        in_specs=[
            pl.BlockSpec((None, nh, ta, wq), lambda bb, hg, j: (bb, hg, j, 0)),
            pl.BlockSpec((None, nh, ta, wq), lambda bb, hg, j: (bb, hg, jnp.minimum(j + 1, nt - 1), 0)),
            pl.BlockSpec((None, nh // 2, t, wq), lambda bb, hg, j: (bb, hg, 0, 0)),
            pl.BlockSpec((None, nh, t // ta, V_ROWS, ta), lambda bb, hg, j: (bb, hg, 0, 0, 0)),
        ],
        out_specs=pl.BlockSpec((None, ta, nh * HEAD_DIM), lambda bb, hg, j: (bb, j, hg)),
        out_shape=jax.ShapeDtypeStruct((b, t, B_WIDTH), F32),
        scratch_shapes=[
            pltpu.VMEM((3, nh, ta, ta), F32),
            pltpu.VMEM((nh, 1, ta), F32),
            pltpu.VMEM((nh, V_ROWS, ta), F32),
        ],
        compiler_params=pltpu.CompilerParams(
            dimension_semantics=("arbitrary", "arbitrary", "arbitrary"), vmem_limit_bytes=VMEM_LIMIT),
        name="fox_prompt",
    )(qaug, qaug, kaug, vt16)


def _proj_sample_kernel(
        x_ref, sh_ref, sc_ref, g_ref, wa_ref, wb_ref, wf_ref, wc_ref,
        gav_ref, gq_ref, gk_ref, bf_ref, gma_ref, gmc_ref, cos_ref, sin_ref,
        wst_ref, bsp_ref, dintra_ref, dq_ref, dk_ref, dch_ref, seg_ref, s0_ref,
        k32_ref, v32_ref, logf_ref, q32_ref, av_ref, oa_ref, oc_ref, sret_ref, *, ns, ts, db):
    x = x_ref[...]
    hb = _modnorm(x, g_ref[...], sh_ref[...], sc_ref[...]).astype(BF16)
    seg = seg_ref[...]

    za = _dot(hb, wa_ref[...])
    a_u = _gelu_tanh(za[:, :A_WIDTH])
    a_vr = _gelu_tanh(za[:, A_WIDTH:])
    a_v = a_vr * lax.rsqrt(jnp.mean(a_vr * a_vr, axis=-1, keepdims=True) + EPS) * gav_ref[...]
    av_ref[...] = a_v

    zb = _dot(hb, wb_ref[...])
    q32_ref[...] = _head_norm(zb[:, :B_WIDTH], seg) * gq_ref[...]
    k32_ref[...] = _head_norm(zb[:, B_WIDTH:2 * B_WIDTH], seg) * gk_ref[...]
    v32_ref[...] = zb[:, 2 * B_WIDTH:]
    logf = _log_sigmoid(_dot(hb, wf_ref[...]) + bf_ref[...])
    logf_ref[...] = logf[:, :B_HEADS]

    zc = _dot(hb, wc_ref[...])
    cos = cos_ref[...]
    sin = sin_ref[...]
    q_c = _rope(zc[:, :C_WIDTH], cos, sin)
    k_c = _rope(zc[:, C_WIDTH:2 * C_WIDTH], cos, sin) * (HEAD_DIM ** -0.5)
    v_c = zc[:, 2 * C_WIDTH:3 * C_WIDTH]
    c_g = zc[:, 3 * C_WIDTH:]

    row = lax.broadcasted_iota(jnp.int32, (ns, ns), 0)
    col = lax.broadcasted_iota(jnp.int32, (ns, ns), 1)
    keep = (row // ts == col // ts) & (col <= row)
    w_cat = jnp.concatenate(
        [jnp.where(keep, wst_ref[h], 0.0) for h in range(A_HEADS)], axis=1).astype(BF16)
    av_blk = _tile_rows_masked(a_v, A_HEADS, ns, HEAD_DIM).astype(BF16)
    o_a = a_u * (_dot(w_cat, av_blk) + bsp_ref[...])

    qc = q_c.astype(BF16)
    k_blk = _tile_rows_masked(k_c, C_HEADS, ns, HEAD_DIM).astype(BF16)
    v_blk = _tile_rows_masked(v_c, C_HEADS, ns, HEAD_DIM).astype(BF16)
    att = _dot_nt(qc, k_blk) * dintra_ref[...]
    intra = _dot(att.astype(BF16), v_blk)
    s0 = s0_ref[...]
    q_exp = _tile_cols_masked(q_c, db, ts, C_WIDTH).astype(BF16)
    inter = _dot(q_exp, s0.astype(BF16)) * dq_ref[...]
    o_c = intra + inter
    kd_t = jnp.transpose(k_c * dk_ref[...])
    kd_exp = _tile_rows_masked(kd_t, db, C_WIDTH, ts).astype(BF16)
    sret_ref[...] = s0 * dch_ref[...] + _dot(kd_exp, v_c.astype(BF16))

    oa_ref[...] = (_head_norm(o_a, seg) * gma_ref[...]).astype(BF16)
    oc_ref[...] = (_head_norm(o_c, seg) * gmc_ref[...] * _silu(c_g)).astype(BF16)


def _proj_sample_call(x2, mod_rows, g, w, cst, s0_blk, *, ts, db):
    ns, d = x2.shape
    in_arrays = [
        x2, mod_rows[:, 3 * d:4 * d], mod_rows[:, 4 * d:5 * d], g,
        w["wa"], w["wb"], w["wf"], w["wc"],
        w["gav"], w["gq"], w["gk"], w["bf"], w["gma"], w["gmc"], cst["cos_s"], cst["sin_s"],
        w["wst"], w["bsp_s"], cst["dintra_s"], cst["dq_s"], cst["dk_s"], cst["dch_s"], cst["seg"], s0_blk,
    ]
    out_shape = [
        jax.ShapeDtypeStruct((ns, B_WIDTH), F32), jax.ShapeDtypeStruct((ns, B_WIDTH), F32),
        jax.ShapeDtypeStruct((ns, B_HEADS), F32), jax.ShapeDtypeStruct((ns, B_WIDTH), F32),
        jax.ShapeDtypeStruct((ns, A_WIDTH), F32),
        jax.ShapeDtypeStruct((ns, A_WIDTH), BF16), jax.ShapeDtypeStruct((ns, C_WIDTH), BF16),
        jax.ShapeDtypeStruct(s0_blk.shape, F32),
    ]
    return pl.pallas_call(
        functools.partial(_proj_sample_kernel, ns=ns, ts=ts, db=db),
        out_shape=out_shape,
        compiler_params=pltpu.CompilerParams(vmem_limit_bytes=VMEM_LIMIT),
        name="mix_proj_sample",
    )(*in_arrays)


def _fox_sample_kernel(pt_ref, q_ref, kn_ref, vn_ref, lfn_ref, triu_ref, *rest, ts, gp, page):
    kp_refs = rest[:gp]
    vp_refs = rest[gp:2 * gp]
    lf_ref = rest[2 * gp]
    o_ref = rest[2 * gp + 1]
    qb_scr, qf_scr, m_scr, l_scr, acc_scr, cr_scr = rest[2 * gp + 2:]
    pg = pl.program_id(1)
    seq = pl.program_id(0)
    nrow = ts * B_HEADS
    hmask = (lax.broadcasted_iota(jnp.int32, (B_HEADS, B_WIDTH), 1) // HEAD_DIM
             == lax.broadcasted_iota(jnp.int32, (B_HEADS, B_WIDTH), 0))

    @pl.when(pg == 0)
    def _():
        q = q_ref[...]
        for t in range(ts):
            qt = jnp.where(hmask, jnp.broadcast_to(q[t:t + 1, :], (B_HEADS, B_WIDTH)), 0.0)
            qf_scr[t * B_HEADS:(t + 1) * B_HEADS, :] = qt
            qb_scr[t * B_HEADS:(t + 1) * B_HEADS, :] = qt.astype(BF16)
        m_scr[...] = jnp.full(m_scr.shape, NEG_BIG, F32)
        l_scr[...] = jnp.zeros_like(l_scr)
        acc_scr[...] = jnp.zeros_like(acc_scr)
        cr_scr[...] = jnp.zeros_like(cr_scr)

    qb = qb_scr[...]
    m = m_scr[...]
    l = l_scr[...]
    acc = acc_scr[...]
    carry = cr_scr[:, 0:1]
    lf_all = jnp.concatenate([lf_ref[pt_ref[seq, pg * gp + g]] for g in range(gp)], axis=0)
    local = _dot_split_lhs(lf_all, triu_ref[...], 3)
    bias = []
    for g in range(gp):
        cum_t = local[g * B_HEADS:(g + 1) * B_HEADS] + carry
        carry = cum_t[:, page - 1:page]
        bias.append(jnp.concatenate([cum_t] * ts, axis=0))
    per = max(1, gp // SAMPLE_PAGE_GROUPS)
    groups = [range(i, i + per) for i in range(0, gp, per)]
    scores = []
    for grp in groups:
        k_grp = jnp.concatenate([kp_refs[g][...].astype(BF16) for g in grp], axis=1)
        scores.append(_dot(qb, k_grp) - jnp.concatenate([bias[g] for g in grp], axis=1))
    parts = []
    for grp, s in zip(groups, scores):
        m_g = jnp.max(s, axis=-1, keepdims=True)
        p = jnp.exp(s - m_g)
        v_grp = jnp.concatenate([vp_refs[g][...].astype(BF16) for g in grp], axis=1)
        parts.append((m_g, jnp.sum(p, axis=-1, keepdims=True), _dot_nt(p.astype(BF16), v_grp)))
    m_new = m
    for m_g, _, _ in parts:
        m_new = jnp.maximum(m_new, m_g)
    alpha = jnp.exp(m - m_new)
    l = alpha * l
    acc = alpha * acc
    for m_g, l_g, a_g in parts:
        w_g = jnp.exp(m_g - m_new)
        l = l + w_g * l_g
        acc = acc + w_g * a_g
    m = m_new
    m_scr[...] = m
    l_scr[...] = l
    acc_scr[...] = acc
    cr_scr[...] = jnp.broadcast_to(carry, cr_scr.shape)

    @pl.when(pg == pl.num_programs(1) - 1)
    def _():
        qf = qf_scr[...]
        kn = kn_ref[...]
        vn = vn_ref[...]
        lfn = lfn_ref[...]
        tok = lax.broadcasted_iota(jnp.int32, (nrow, 1), 0) // B_HEADS
        run = carry
        s_new = []
        for jn in range(ts):
            run = run + lfn[:, jn:jn + 1]
            sj = jnp.sum(qf * kn[jn:jn + 1, :], axis=-1, keepdims=True) - jnp.concatenate([run] * ts, axis=0)
            s_new.append(jnp.where(tok >= jn, sj, NEG_BIG))
        m2 = m
        for sj in s_new:
            m2 = jnp.maximum(m2, sj)
        alpha = jnp.exp(m - m2)
        l2 = alpha * l
        acc2 = alpha * acc
        for jn in range(ts):
            pj = jnp.exp(s_new[jn] - m2)
            l2 = l2 + pj
            acc2 = acc2 + pj * vn[jn:jn + 1, :]
        o_full = acc2 / l2
        for t in range(ts):
            blk = jnp.where(hmask, o_full[t * B_HEADS:(t + 1) * B_HEADS, :], 0.0)
            o_ref[t:t + 1, :] = jnp.sum(blk, axis=0, keepdims=True)


def _fox_sample_call(page_table, q3, kn3, vn3, lfn_t, triu, kc, vc, lfc_t, layer, *, gp):
    db, ts, _ = q3.shape
    n_pages = page_table.shape[1]
    page = kc.shape[3]
    nrow = ts * B_HEADS
    seq3 = lambda shape: pl.BlockSpec((None,) + shape, lambda b, pg, pt: (b, 0, 0))

    def page_spec(shape, g):
        return pl.BlockSpec((None, None) + shape, lambda b, pg, pt, g=g: (layer, pt[b, pg * gp + g], 0, 0))

    in_specs = [seq3((ts, B_WIDTH)), seq3((ts, B_WIDTH)), seq3((ts, B_WIDTH)), seq3((B_HEADS, ts)),
                pl.BlockSpec((page, page), lambda b, pg, pt: (0, 0))]
    in_specs += [page_spec((B_WIDTH, page), g) for g in range(gp)]
    in_specs += [page_spec((B_WIDTH, page), g) for g in range(gp)]
    in_specs += [pl.BlockSpec((None,) + lfc_t.shape[1:], lambda b, pg, pt: (layer, 0, 0, 0),
                              pipeline_mode=pl.Buffered(1))]
    grid_spec = pltpu.PrefetchScalarGridSpec(
        num_scalar_prefetch=1,
        grid=(db, n_pages // gp),
        in_specs=in_specs,
        out_specs=pl.BlockSpec((None, ts, B_WIDTH), lambda b, pg, pt: (b, 0, 0)),
        scratch_shapes=[
            pltpu.VMEM((nrow, B_WIDTH), BF16), pltpu.VMEM((nrow, B_WIDTH), F32),
            pltpu.VMEM((nrow, 1), F32), pltpu.VMEM((nrow, 1), F32),
            pltpu.VMEM((nrow, B_WIDTH), F32), pltpu.VMEM((B_HEADS, LANES), F32),
        ],
    )
    return pl.pallas_call(
        functools.partial(_fox_sample_kernel, ts=ts, gp=gp, page=page),
        grid_spec=grid_spec,
        out_shape=jax.ShapeDtypeStruct((db, ts, B_WIDTH), F32),
        compiler_params=pltpu.CompilerParams(
            dimension_semantics=("arbitrary", "arbitrary"), vmem_limit_bytes=VMEM_LIMIT),
        name="fox_sample",
    )(page_table, q3, kn3, vn3, lfn_t, triu, *([kc] * gp), *([vc] * gp), lfc_t)


def _rope_tables(pos):
    half = HEAD_DIM // 2
    inv = ROPE_BASE ** (-jnp.arange(half, dtype=F32) / half)
    ang = pos[:, None] * inv[None, :]
    cos = jnp.cos(ang)
    sin = jnp.sin(ang)
    cos_h = jnp.concatenate([cos, cos], axis=-1)
    sin_h = jnp.concatenate([-sin, sin], axis=-1)
    return jnp.tile(cos_h, (1, C_HEADS)), jnp.tile(sin_h, (1, C_HEADS))


def _decay_tables(length, log_gamma):
    idx = jnp.arange(length, dtype=F32)
    rel = idx[:, None] - idx[None, :]
    d_intra = jnp.where(rel[None] >= 0, jnp.exp(jnp.maximum(rel, 0.0)[None] * log_gamma[:, None, None]), 0.0)
    d_q = jnp.exp((idx + 1.0)[:, None] * log_gamma[None, :])
    d_k = jnp.exp((length - 1.0 - idx)[:, None] * log_gamma[None, :])
    d_chunk = jnp.exp(length * log_gamma)
    return d_intra, d_q, d_k, d_chunk


def _constants(t, ts, db, past, tm):
    log_gamma = jnp.log1p(-jnp.exp2(-5.0 - jnp.arange(C_HEADS, dtype=F32)))
    cst = {}
    cst["cos_p"], cst["sin_p"] = _rope_tables(jnp.arange(t, dtype=F32))
    cos_s, sin_s = _rope_tables(past + jnp.arange(ts, dtype=F32))
    cst["cos_s"], cst["sin_s"] = jnp.tile(cos_s, (db, 1)), jnp.tile(sin_s, (db, 1))
    rep = lambda a: jnp.repeat(a, HEAD_DIM, axis=-1)

    d_intra, d_q, d_k, d_chunk = _decay_tables(CHUNK, log_gamma)
    cst["dintra_p"] = jnp.concatenate([d_intra[h] for h in range(C_HEADS)], axis=1)
    cst["dq_p"], cst["dk_p"], cst["dch_p"] = rep(d_q), rep(d_k), rep(d_chunk[None, :])

    d_intra, d_q, d_k, d_chunk = _decay_tables(ts, log_gamma)
    same_seq = jnp.kron(jnp.eye(db, dtype=F32), jnp.ones((ts, ts), F32))
    cst["dintra_s"] = jnp.concatenate(
        [jnp.tile(d_intra[h], (db, db)) * same_seq for h in range(C_HEADS)], axis=1)
    cst["dq_s"] = jnp.tile(rep(d_q), (db, 1))
    cst["dk_s"] = jnp.tile(rep(d_k), (db, 1))
    cst["dch_s"] = rep(d_chunk[None, :])

    seg = jnp.kron(jnp.eye(MXU_DIM // HEAD_DIM, dtype=F32), jnp.ones((HEAD_DIM, HEAD_DIM), F32))
    cst["seg"] = seg.astype(BF16)
    cst["tril"] = jnp.tril(jnp.ones((tm, tm), F32)).astype(BF16)

    nb = NPIECE * B_HEADS
    place = np.zeros((NPIECE * LANES, 2 * LANES), np.float32)
    pconst = np.zeros((1, 2 * LANES), np.float32)
    pconst[0, :nb] = 1.0
    pconst[0, LANES + nb:LANES + 2 * nb] = -1.0
    for h in range(B_HEADS):
        for a in range(NPIECE):
            place[a * LANES + h, nb + NPIECE * h + a] = 1.0
            place[a * LANES + h, LANES + NPIECE * h + a] = 1.0
    cst["place"], cst["pconst"] = jnp.asarray(place, BF16), jnp.asarray(pconst)
    return cst


def _layer_weights(l, w_in, g_a_v, w_spatial, b_spatial, g_qnorm, g_knorm, b_forget, g_mix, ts, db):
    d = w_in.shape[1]
    wl = w_in[l]
    o = 0
    wa = wl[:, o:o + 2 * A_WIDTH]; o += 2 * A_WIDTH
    wb = wl[:, o:o + 3 * B_WIDTH]; o += 3 * B_WIDTH
    wf = wl[:, o:o + B_HEADS]; o += B_HEADS
    wc = wl[:, o:o + 4 * C_WIDTH]
    w = {
        "wa": wa.astype(BF16), "wb": wb.astype(BF16), "wc": wc.astype(BF16),
        "wf": jnp.pad(wf, ((0, 0), (0, LANES - B_HEADS))).astype(BF16),
        "gav": g_a_v[l][None, :],
        "gq": jnp.tile(g_qnorm[l], B_HEADS)[None, :] * (HEAD_DIM ** -0.5),
        "gk": jnp.tile(g_knorm[l], B_HEADS)[None, :],
        "bf": jnp.pad(b_forget[l], (0, LANES - B_HEADS))[None, :],
        "gma": g_mix[l][None, :A_WIDTH],
        "gmb": g_mix[l][None, A_WIDTH:A_WIDTH + B_WIDTH],
        "gmc": g_mix[l][None, A_WIDTH + B_WIDTH:],
        "ws": w_spatial[l],
        "bsp_p": jnp.repeat(b_spatial[l].T, HEAD_DIM, axis=-1),
        "wst": jnp.tile(w_spatial[l][:, :ts, :ts], (1, db, db)),
        "bsp_s": jnp.tile(jnp.repeat(b_spatial[l][:, :ts].T, HEAD_DIM, axis=-1), (db, 1)),
    }
    del d
    return w


def kernel(x_prompt, x_sample, cache_k, cache_v, cache_logf, state_ret, page_table, c_prompt, c_sample,
           g_norm, w_ada, b_ada, w_ffn_gate, w_ffn_up, w_ffn_down, w_in, g_a_v, w_spatial, b_spatial,
           g_qnorm, g_knorm, b_forget, g_mix, w_out):
    b, t, d = x_prompt.shape
    db, ts, _ = x_sample.shape
    depth = w_in.shape[0]
    ns = db * ts
    n_pool, page = cache_k.shape[1], cache_k.shape[2]
    n_pages = page_table.shape[1]
    past = n_pages * page

    tm = min(512, t)
    ta = min(512, t)
    nf = 2
    gp = min(16, n_pages)

    cst = _constants(t, ts, db, past, tm)
    triu_page = jnp.triu(jnp.ones((page, page), F32)).astype(BF16)

    mod = _ada_call(jnp.concatenate([c_prompt, c_sample], axis=0), w_ada, b_ada)

    kc = jnp.transpose(cache_k, (0, 1, 3, 4, 2)).reshape(depth, n_pool, B_WIDTH, page)
    vc = jnp.transpose(cache_v, (0, 1, 3, 4, 2)).reshape(depth, n_pool, B_WIDTH, page)
    lfc_t = jnp.swapaxes(cache_logf, 2, 3)

    ffn_w = (w_ffn_gate.astype(BF16), w_ffn_up.astype(BF16), w_ffn_down.astype(BF16))
    w_out_bf = w_out.astype(BF16)

    x_p = x_prompt.reshape(b * t, d)
    x_s = x_sample.reshape(ns, d)
    outs = {k: [] for k in ("lp", "rp", "ks", "vs", "ls", "rs", "av")}
    kv_prev = None
    head_idx = jnp.arange(C_HEADS)
    for l in range(depth):
        mod_p = mod[l, :b].reshape(b, 1, -1)
        mod_s = jnp.repeat(mod[l, b:], ts, axis=0)
        w = _layer_weights(l, w_in, g_a_v, w_spatial, b_spatial, g_qnorm, g_knorm, b_forget, g_mix, ts, db)
        prompt_cfg = dict(per_row=False, tm=tm, seq_len=t, nf=nf)
        sample_cfg = dict(per_row=True, tm=ns, seq_len=ts, nf=nf)

        x_p = _ffn_call(x_p, mod_p, 0, g_norm[l, 0][None, :], *ffn_w, l, 0, **prompt_cfg)
        x_s = _ffn_call(x_s, mod_s, 0, g_norm[l, 0][None, :], *ffn_w, l, 0, **sample_cfg)

        (kt32, vt32, logf, qaug, kaug, vt16, oa, oc, sret) = _proj_prompt_call(
            x_p.reshape(b, t, d), mod_p, g_norm[l, 1][None, :], w, cst, kv_prev, tm=tm, ta=ta)
        kv_prev = (kt32, vt32)
        ob = _fox_prompt_call(qaug, kaug, vt16, ta=ta, nh=FOX_HEADS_PER_STEP)
        merge_p = (oa.reshape(b * t, -1), ob.reshape(b * t, -1), oc.reshape(b * t, -1),
                   w["gmb"], cst["seg"], w_out_bf)
        outs["lp"].append(logf)
        s5 = sret.reshape(b, C_HEADS, HEAD_DIM, C_HEADS, HEAD_DIM)
        outs["rp"].append(jnp.moveaxis(s5[:, head_idx, :, head_idx, :], 0, 1))

        s0_blk = jnp.einsum("bhde,hg->bhdge", state_ret[l], jnp.eye(C_HEADS, dtype=F32)).reshape(
            db * C_WIDTH, C_WIDTH)
        (k32s, v32s, logfs, q32s, avs, oas, ocs, srets) = _proj_sample_call(
            x_s, mod_s, g_norm[l, 1][None, :], w, cst, s0_blk, ts=ts, db=db)
        obs = _fox_sample_call(
            page_table, q32s.reshape(db, ts, -1), k32s.reshape(db, ts, -1), v32s.reshape(db, ts, -1),
            jnp.swapaxes(logfs.reshape(db, ts, B_HEADS), 1, 2), triu_page, kc, vc, lfc_t, l, gp=gp)
        merge_s = (oas, obs.reshape(ns, -1), ocs, w["gmb"], cst["seg"], w_out_bf)
        outs["ks"].append(k32s.reshape(db, ts, B_HEADS, HEAD_DIM))
        outs["vs"].append(v32s.reshape(db, ts, B_HEADS, HEAD_DIM))
        outs["ls"].append(logfs.reshape(db, ts, B_HEADS))
        s5 = srets.reshape(db, C_HEADS, HEAD_DIM, C_HEADS, HEAD_DIM)
        outs["rs"].append(jnp.moveaxis(s5[:, head_idx, :, head_idx, :], 0, 1))
        outs["av"].append(avs.reshape(db, ts, A_WIDTH))

        x_p = _ffn_call(x_p, mod_p, 2, g_norm[l, 2][None, :], *ffn_w, l, 1, merge=merge_p, **prompt_cfg)
        x_s = _ffn_call(x_s, mod_s, 2, g_norm[l, 2][None, :], *ffn_w, l, 1, merge=merge_s, **sample_cfg)

    st = lambda key: jnp.stack(outs[key])
    kv_out = lambda a: jnp.transpose(a.reshape(depth, b, B_HEADS, HEAD_DIM, t), (0, 1, 4, 2, 3))
    return (x_p.reshape(b, t, d), x_s.reshape(db, ts, d), kv_out(kv_prev[0]), kv_out(kv_prev[1]), st("lp"), st("rp"),
            st("ks"), st("vs"), st("ls"), st("rs"), st("av"))
```

```python
import functools

import numpy as np
import jax
import jax.numpy as jnp
from jax import lax
from jax.experimental import pallas as pl
from jax.experimental.pallas import tpu as pltpu

F32 = jnp.float32
BF16 = jnp.bfloat16

HEAD_DIM = 64
A_HEADS, B_HEADS, C_HEADS = 4, 8, 4
A_WIDTH, B_WIDTH, C_WIDTH = A_HEADS * HEAD_DIM, B_HEADS * HEAD_DIM, C_HEADS * HEAD_DIM
CHUNK = 128
ROPE_BASE = 10000.0
EPS = 1e-6
N_SUBLAYERS = 3
LANES = 128
MXU_DIM = 256
NEG_BIG = -1e30
FOX_HEADS_PER_STEP = 4
SAMPLE_PAGE_GROUPS = 8
NPIECE = 3
V_ROWS = 80
LOG2E = 1.4426950408889634
VMEM_LIMIT = 56 * 1024 * 1024


def _dot(a, b):
    return jnp.dot(a, b, preferred_element_type=F32)


def _dot_nt(a, b):
    return lax.dot_general(a, b, (((1,), (1,)), ((), ())), preferred_element_type=F32)


def _silu(x):
    return x / (1.0 + jnp.exp(-x))


def _gelu_tanh(x):
    c = 0.7978845608028654
    return 0.5 * x * (1.0 + jnp.tanh(c * (x + 0.044715 * (x * x * x))))


def _log_sigmoid(x):
    return jnp.minimum(x, 0.0) - jnp.log1p(jnp.exp(-jnp.abs(x)))


def _modnorm(x, g, shift, scale):
    ms = jnp.mean(x * x, axis=-1, keepdims=True)
    return (x * lax.rsqrt(ms + EPS) * g) * (1.0 + scale) + shift


def _split_bf16(x, parts):
    out = []
    r = x
    for i in range(parts):
        p = r.astype(BF16)
        out.append(p)
        if i + 1 < parts:
            r = r - p.astype(F32)
    return out


def _dot_split_lhs(a, b_bf16, parts):
    acc = None
    for p in _split_bf16(a, parts):
        t = _dot(p, b_bf16)
        acc = t if acc is None else acc + t
    return acc


def _dot_split_rhs(a_bf16, b, parts):
    acc = None
    for p in _split_bf16(b, parts):
        t = _dot(a_bf16, p)
        acc = t if acc is None else acc + t
    return acc


def _head_mean_sq(x, seg):
    w = x.shape[-1]
    outs = []
    for c in range(w // MXU_DIM):
        xs = x[:, c * MXU_DIM:(c + 1) * MXU_DIM]
        outs.append(_dot_split_lhs(xs * xs, seg, 1))
    ss = outs[0] if len(outs) == 1 else jnp.concatenate(outs, axis=-1)
    return ss * (1.0 / HEAD_DIM)


def _head_norm(x, seg):
    return x * lax.rsqrt(_head_mean_sq(x, seg) + EPS)


def _rope(x, cos, sin_signed):
    outs = []
    for c in range(x.shape[-1] // LANES):
        xs = x[:, c * LANES:(c + 1) * LANES]
        fwd = pltpu.roll(xs, HEAD_DIM // 2, axis=1)
        bwd = pltpu.roll(xs, LANES - HEAD_DIM // 2, axis=1)
        lane = lax.broadcasted_iota(jnp.int32, xs.shape, 1)
        outs.append(jnp.where((lane % HEAD_DIM) < HEAD_DIM // 2, bwd, fwd))
    partner = jnp.concatenate(outs, axis=-1)
    return x * cos + partner * sin_signed


def _tile_rows_masked(x, nrep, rows_per, cols_per):
    xt = jnp.concatenate([x] * nrep, axis=0)
    r = lax.broadcasted_iota(jnp.int32, xt.shape, 0) // rows_per
    c = lax.broadcasted_iota(jnp.int32, xt.shape, 1) // cols_per
    return jnp.where(r == c, xt, 0.0)


def _tile_cols_masked(x, nrep, rows_per, cols_per):
    xt = jnp.concatenate([x] * nrep, axis=1)
    r = lax.broadcasted_iota(jnp.int32, xt.shape, 0) // rows_per
    c = lax.broadcasted_iota(jnp.int32, xt.shape, 1) // cols_per
    return jnp.where(r == c, xt, 0.0)


def _ada_kernel(c_ref, w_ref, b_ref, o_ref):
    a = _silu(c_ref[...]).astype(BF16)
    o_ref[...] = _dot(a, w_ref[...].astype(BF16)) + b_ref[...]


def _ada_call(c_all, w_ada, b_ada):
    depth, d, ncol = w_ada.shape
    rows = c_all.shape[0]
    tn = 1024
    return pl.pallas_call(
        _ada_kernel,
        grid=(depth, ncol // tn),
        in_specs=[
            pl.BlockSpec((rows, d), lambda l, n: (0, 0)),
            pl.BlockSpec((None, d, tn), lambda l, n: (l, 0, n)),
            pl.BlockSpec((None, 1, tn), lambda l, n: (l, 0, n)),
        ],
        out_specs=pl.BlockSpec((None, rows, tn), lambda l, n: (l, 0, n)),
        out_shape=jax.ShapeDtypeStruct((depth, rows, ncol), F32),
        compiler_params=pltpu.CompilerParams(
            dimension_semantics=("arbitrary", "arbitrary"), vmem_limit_bytes=VMEM_LIMIT),
        name="adaln",
    )(c_all, w_ada, b_ada.reshape(depth, 1, ncol))


def _mod_specs(per_row, tm, d, tiles_per_seq, sub, which):
    col = sub * 3 + which
    if per_row:
        return pl.BlockSpec((tm, d), lambda i: (i, col))
    return pl.BlockSpec((None, 1, d), lambda i: (i // tiles_per_seq, 0, col))


def _half_ffn(x, shift, scale, gate_mod, g, wg_ref, wu_ref, wd_ref, nf):
    d_ff = wg_ref.shape[1]
    tf = d_ff // nf
    h = _modnorm(x, g, shift, scale).astype(BF16)
    acc = None
    for f in range(nf):
        cols = slice(f * tf, (f + 1) * tf)
        gate = _dot(h, wg_ref[:, cols])
        up = _dot(h, wu_ref[:, cols])
        a = (_silu(gate) * up).astype(BF16)
        t = _dot(a, wd_ref[cols, :])
        acc = t if acc is None else acc + t
    return x + (0.5 * gate_mod) * acc


def _ffn_kernel(x_ref, sh_ref, sc_ref, gt_ref, g_ref, wg_ref, wu_ref, wd_ref, o_ref, *, nf):
    o_ref[...] = _half_ffn(x_ref[...], sh_ref[...], sc_ref[...], gt_ref[...], g_ref[...],
                           wg_ref, wu_ref, wd_ref, nf)


def _merge_ffn_kernel(oa_ref, ob_ref, oc_ref, x_ref, gt1_ref, gmb_ref, seg_ref, wo_ref,
                      sh_ref, sc_ref, gt_ref, g_ref, wg_ref, wu_ref, wd_ref, o_ref, *, nf):
    ob = (_head_norm(ob_ref[...], seg_ref[...]) * gmb_ref[...]).astype(BF16)
    ab = A_WIDTH + B_WIDTH
    y = (_dot(oa_ref[...], wo_ref[:A_WIDTH, :]) + _dot(ob, wo_ref[A_WIDTH:ab, :])
         + _dot(oc_ref[...], wo_ref[ab:, :]))
    x = x_ref[...] + gt1_ref[...] * y
    o_ref[...] = _half_ffn(x, sh_ref[...], sc_ref[...], gt_ref[...], g_ref[...], wg_ref, wu_ref, wd_ref, nf)


def _ffn_call(x2, mod, sub, g, wg, wu, wd, layer, which, *, per_row, tm, seq_len, nf, merge=None):
    n, d = x2.shape
    d_ff = wg.shape[-1]
    tiles_per_seq = seq_len // tm if not per_row else 1
    const2 = lambda i: (0, 0)
    rows = lambda width: pl.BlockSpec((tm, width), lambda i: (i, 0))
    ffn_w = lambda shape: pl.BlockSpec((None, None) + shape, lambda i: (layer, which, 0, 0),
                                       pipeline_mode=pl.Buffered(1))
    ffn_specs = [
        _mod_specs(per_row, tm, d, tiles_per_seq, sub, 0),
        _mod_specs(per_row, tm, d, tiles_per_seq, sub, 1),
        _mod_specs(per_row, tm, d, tiles_per_seq, sub, 2),
        pl.BlockSpec((1, d), const2),
        ffn_w((d, d_ff)), ffn_w((d, d_ff)), ffn_w((d_ff, d)),
    ]
    ffn_args = (mod, mod, mod, g, wg, wu, wd)
    if merge is None:
        body, in_specs, args = _ffn_kernel, [rows(d)] + ffn_specs, (x2,) + ffn_args
    else:
        oa, ob, oc, gmb, seg, w_out = merge
        body = _merge_ffn_kernel
        in_specs = [rows(A_WIDTH), rows(B_WIDTH), rows(C_WIDTH), rows(d),
                    _mod_specs(per_row, tm, d, tiles_per_seq, 1, 2),
                    pl.BlockSpec((1, B_WIDTH), const2), pl.BlockSpec((MXU_DIM, MXU_DIM), const2),
                    pl.BlockSpec((None, d, d), lambda i: (layer, 0, 0), pipeline_mode=pl.Buffered(1))] + ffn_specs
        args = (oa, ob, oc, x2, mod, gmb, seg, w_out) + ffn_args
    return pl.pallas_call(
        functools.partial(body, nf=nf),
        grid=(n // tm,),
        in_specs=in_specs,
        out_specs=rows(d),
        out_shape=jax.ShapeDtypeStruct((n, d), F32),
        compiler_params=pltpu.CompilerParams(
            dimension_semantics=("arbitrary",), vmem_limit_bytes=VMEM_LIMIT),
        name="half_ffn" if merge is None else "merge_ffn",
    )(*args)


def _proj_prompt_kernel(
        x_ref, sh_ref, sc_ref, g_ref, wa_ref, wb_ref, wf_ref, wc_ref,
        gav_ref, gq_ref, gk_ref, bf_ref, gma_ref, gmc_ref, cos_ref, sin_ref,
        ws_ref, bsp_ref, dintra_ref, dq_ref, dk_ref, dch_ref, seg_ref, tril_ref,
        place_ref, pconst_ref, *rest, tm, ta, n_prev):
    n_in = 2 if n_prev else 0
    prev_refs = rest[:n_in]
    (kt32_ref, vt32_ref, logf_ref, qaug_ref, kaug_ref, vt16_ref,
     oa_ref, oc_ref, sret_ref, s_scr, cc_scr) = rest[n_in:]
    j = pl.program_id(1)

    @pl.when(j == 0)
    def _():
        s_scr[...] = jnp.zeros_like(s_scr)
        cc_scr[...] = jnp.zeros_like(cc_scr)

    x = x_ref[...]
    hb = _modnorm(x, g_ref[...], sh_ref[...], sc_ref[...]).astype(BF16)
    seg = seg_ref[...]

    za = _dot(hb, wa_ref[...])
    zb = _dot(hb, wb_ref[...])
    zf = _dot(hb, wf_ref[...])
    zc = _dot(hb, wc_ref[...])

    a_u = _gelu_tanh(za[:, :A_WIDTH])
    a_vr = _gelu_tanh(za[:, A_WIDTH:])
    a_v = a_vr * lax.rsqrt(jnp.mean(a_vr * a_vr, axis=-1, keepdims=True) + EPS) * gav_ref[...]

    q_b = _head_norm(zb[:, :B_WIDTH], seg) * gq_ref[...]
    k_b = _head_norm(zb[:, B_WIDTH:2 * B_WIDTH], seg) * gk_ref[...]
    v_b = zb[:, 2 * B_WIDTH:]
    for i in range(n_prev):
        kt32_ref[i] = prev_refs[0][i]
        vt32_ref[i] = prev_refs[1][i]
    kt32_ref[n_prev] = jnp.transpose(k_b)
    v_t = jnp.transpose(v_b)
    vt32_ref[n_prev] = v_t
    v_t16 = v_t.astype(BF16)
    pad_row = lax.broadcasted_iota(jnp.int32, (V_ROWS - HEAD_DIM, ta), 0)
    ones_pad = jnp.where(pad_row == 0, 1.0, 0.0).astype(BF16)
    for h in range(B_HEADS):
        for c in range(tm // ta):
            vt16_ref[h, c] = jnp.concatenate(
                [v_t16[h * HEAD_DIM:(h + 1) * HEAD_DIM, c * ta:(c + 1) * ta], ones_pad], axis=0)

    logf = _log_sigmoid(zf + bf_ref[...])
    logf_ref[...] = logf[:, :B_HEADS]
    cum_c = _dot_split_rhs(tril_ref[...], logf, 3) + cc_scr[...]
    cc_scr[...] = cum_c[tm - 1:tm, :]
    pieces = jnp.concatenate(_split_bf16(cum_c * LOG2E, NPIECE), axis=-1)
    aux = (_dot(pieces, place_ref[...]) + pconst_ref[...]).astype(BF16)
    k_aux = aux[:, :LANES]
    q_all = aux[:, LANES:]
    q16 = (q_b * LOG2E).astype(BF16)
    k16 = k_b.astype(BF16)
    lane = lax.broadcasted_iota(jnp.int32, (tm, LANES), 1)
    zero16 = jnp.zeros((tm, LANES), BF16)
    for hp in range(B_HEADS // 2):
        pair = slice(hp * LANES, (hp + 1) * LANES)
        kaug_ref[hp, :, :LANES] = k16[:, pair]
        kaug_ref[hp, :, LANES:] = k_aux
        for i in range(2):
            h = 2 * hp + i
            own = (lane < HEAD_DIM) if i == 0 else (lane >= HEAD_DIM)
            qaug_ref[h, :, :LANES] = jnp.where(own, q16[:, pair], zero16)
            rel = lane - NPIECE * h
            mine = ((rel >= 0) & (rel < NPIECE)) | ((rel >= NPIECE * B_HEADS) & (rel < NPIECE * (B_HEADS + 1)))
            qaug_ref[h, :, LANES:] = jnp.where(mine, q_all, zero16)

    cos = cos_ref[...]
    sin = sin_ref[...]
    q_c = _rope(zc[:, :C_WIDTH], cos, sin)
    k_c = _rope(zc[:, C_WIDTH:2 * C_WIDTH], cos, sin) * (HEAD_DIM ** -0.5)
    v_c = zc[:, 2 * C_WIDTH:3 * C_WIDTH]
    c_g = zc[:, 3 * C_WIDTH:]

    row = lax.broadcasted_iota(jnp.int32, (CHUNK, CHUNK), 0)
    col = lax.broadcasted_iota(jnp.int32, (CHUNK, CHUNK), 1)
    w_cat = jnp.concatenate(
        [jnp.where(col <= row, ws_ref[h], 0.0) for h in range(A_HEADS)], axis=1).astype(BF16)
    dintra = dintra_ref[...]
    dq = dq_ref[...]
    dk = dk_ref[...]
    dch = dch_ref[...]
    bsp = bsp_ref[...]
    sr = lax.broadcasted_iota(jnp.int32, (C_WIDTH, C_WIDTH), 0) // HEAD_DIM
    scol = lax.broadcasted_iota(jnp.int32, (C_WIDTH, C_WIDTH), 1) // HEAD_DIM
    s_mask = sr == scol
    s_blk = s_scr[...]
    oa_parts = []
    oc_parts = []
    for c in range(tm // CHUNK):
        sl = slice(c * CHUNK, (c + 1) * CHUNK)
        av_blk = _tile_rows_masked(a_v[sl], A_HEADS, CHUNK, HEAD_DIM).astype(BF16)
        oa_parts.append(a_u[sl] * (_dot(w_cat, av_blk) + bsp))

        qc = q_c[sl].astype(BF16)
        k_blk = _tile_rows_masked(k_c[sl], C_HEADS, CHUNK, HEAD_DIM).astype(BF16)
        v_blk = _tile_rows_masked(v_c[sl], C_HEADS, CHUNK, HEAD_DIM).astype(BF16)
        att = _dot_nt(qc, k_blk) * dintra
        intra = _dot(att.astype(BF16), v_blk)
        inter = _dot(qc, s_blk.astype(BF16)) * dq
        oc_parts.append(intra + inter)
        kd_t = jnp.transpose(k_c[sl] * dk).astype(BF16)
        upd = _dot(kd_t, v_c[sl].astype(BF16))
        s_blk = s_blk * dch + jnp.where(s_mask, upd, 0.0)
    s_scr[...] = s_blk
    folded = s_blk[:, :HEAD_DIM]
    for h in range(1, C_HEADS):
        folded = folded + s_blk[:, h * HEAD_DIM:(h + 1) * HEAD_DIM]
    sret_ref[...] = folded

    o_a = oa_parts[0] if len(oa_parts) == 1 else jnp.concatenate(oa_parts, axis=0)
    o_c = oc_parts[0] if len(oc_parts) == 1 else jnp.concatenate(oc_parts, axis=0)
    oa_ref[...] = (_head_norm(o_a, seg) * gma_ref[...]).astype(BF16)
    oc_ref[...] = (_head_norm(o_c, seg) * gmc_ref[...] * _silu(c_g)).astype(BF16)


def _proj_prompt_call(x3, mod, g, w, cst, kv_prev, *, tm, ta):
    b, t, d = x3.shape
    n_prev = 0 if kv_prev is None else kv_prev[0].shape[0]
    nt = t // tm
    full2 = lambda bb, j: (0, 0)
    full3 = lambda bb, j: (0, 0, 0)
    row_spec = lambda width: pl.BlockSpec((None, tm, width), lambda bb, j: (bb, j, 0))
    mod_spec = lambda col: pl.BlockSpec((None, 1, d), lambda bb, j: (bb, 0, col))
    vec = lambda width: pl.BlockSpec((1, width), full2)
    in_specs = [
        row_spec(d), mod_spec(3), mod_spec(4), vec(d),
        pl.BlockSpec((d, 2 * A_WIDTH), full2), pl.BlockSpec((d, 3 * B_WIDTH), full2),
        pl.BlockSpec((d, LANES), full2), pl.BlockSpec((d, 4 * C_WIDTH), full2),
        vec(A_WIDTH), vec(B_WIDTH), vec(B_WIDTH), vec(LANES), vec(A_WIDTH), vec(C_WIDTH),
        pl.BlockSpec((tm, C_WIDTH), lambda bb, j: (j, 0)),
        pl.BlockSpec((tm, C_WIDTH), lambda bb, j: (j, 0)),
        pl.BlockSpec((A_HEADS, CHUNK, CHUNK), full3),
        pl.BlockSpec((CHUNK, A_WIDTH), full2),
        pl.BlockSpec((CHUNK, C_HEADS * CHUNK), full2),
        pl.BlockSpec((CHUNK, C_WIDTH), full2), pl.BlockSpec((CHUNK, C_WIDTH), full2),
        vec(C_WIDTH),
        pl.BlockSpec((MXU_DIM, MXU_DIM), full2),
        pl.BlockSpec((tm, tm), full2),
        pl.BlockSpec((NPIECE * LANES, 2 * LANES), full2), vec(2 * LANES),
    ]
    nhp = B_HEADS // 2
    stack_spec = lambda n: pl.BlockSpec((n, None, B_WIDTH, tm), lambda bb, j: (0, bb, 0, j))
    prev_args = ()
    if n_prev:
        in_specs += [stack_spec(n_prev), stack_spec(n_prev)]
        prev_args = tuple(kv_prev)
    col_spec = stack_spec(n_prev + 1)
    out_shape = [
        jax.ShapeDtypeStruct((n_prev + 1, b, B_WIDTH, t), F32), jax.ShapeDtypeStruct((n_prev + 1, b, B_WIDTH, t), F32),
        jax.ShapeDtypeStruct((b, t, B_HEADS), F32),
        jax.ShapeDtypeStruct((b, B_HEADS, t, 2 * LANES), BF16),
        jax.ShapeDtypeStruct((b, nhp, t, 2 * LANES), BF16),
        jax.ShapeDtypeStruct((b, B_HEADS, t // ta, V_ROWS, ta), BF16),
        jax.ShapeDtypeStruct((b, t, A_WIDTH), BF16), jax.ShapeDtypeStruct((b, t, C_WIDTH), BF16),
        jax.ShapeDtypeStruct((b, C_WIDTH, HEAD_DIM), F32),
    ]
    out_specs = [
        col_spec, col_spec, row_spec(B_HEADS),
        pl.BlockSpec((None, B_HEADS, tm, 2 * LANES), lambda bb, j: (bb, 0, j, 0)),
        pl.BlockSpec((None, nhp, tm, 2 * LANES), lambda bb, j: (bb, 0, j, 0)),
        pl.BlockSpec((None, B_HEADS, tm // ta, V_ROWS, ta), lambda bb, j: (bb, 0, j, 0, 0)),
        row_spec(A_WIDTH), row_spec(C_WIDTH),
        pl.BlockSpec((None, C_WIDTH, HEAD_DIM), lambda bb, j: (bb, 0, 0)),
    ]
    return pl.pallas_call(
        functools.partial(_proj_prompt_kernel, tm=tm, ta=ta, n_prev=n_prev),
        grid=(b, nt),
        in_specs=in_specs,
        out_specs=out_specs,
        out_shape=out_shape,
        scratch_shapes=[
            pltpu.VMEM((C_WIDTH, C_WIDTH), F32),
            pltpu.VMEM((1, LANES), F32),
        ],
        compiler_params=pltpu.CompilerParams(
            dimension_semantics=("arbitrary", "arbitrary"), vmem_limit_bytes=VMEM_LIMIT),
        name="mix_proj_prompt",
    )(x3, mod, mod, g, w["wa"], w["wb"], w["wf"], w["wc"],
      w["gav"], w["gq"], w["gk"], w["bf"], w["gma"], w["gmc"], cst["cos_p"], cst["sin_p"],
      w["ws"], w["bsp_p"], cst["dintra_p"], cst["dq_p"], cst["dk_p"], cst["dch_p"],
      cst["seg"], cst["tril"], cst["place"], cst["pconst"], *prev_args)


def _fox_prompt_kernel(q_ref, qn_ref, k_ref, vt_ref, o_ref, s_scr, m_scr, acc_scr, *, ta, nh):
    j = pl.program_id(2)
    kpos = lax.broadcasted_iota(jnp.int32, (ta, ta), 0)
    qpos = lax.broadcasted_iota(jnp.int32, (ta, ta), 1)
    causal = kpos <= qpos

    def scores_into(slot, kb, queries=q_ref):
        start = pl.multiple_of(kb * ta, ta)
        for hp in range(nh // 2):
            kblk = k_ref[hp, pl.ds(start, ta), :]
            for h in (2 * hp, 2 * hp + 1):
                s_scr[slot, h] = _dot_nt(kblk, queries[h])

    def prefetch_next_tile():
        scores_into(2, 0, qn_ref)

    def consume(slot, kb, masked):
        for h in range(nh):
            s = s_scr[slot, h]
            if masked:
                s = jnp.where(causal, s, NEG_BIG)
            m = m_scr[h]
            m_new = jnp.maximum(m, jnp.max(s, axis=0, keepdims=True))
            p = jnp.exp2(s - m_new)
            alpha = jnp.exp2(m - m_new)
            acc_scr[h] = alpha * acc_scr[h] + _dot(vt_ref[h, kb], p.astype(BF16))
            m_scr[h] = m_new

    m_scr[...] = jnp.full(m_scr.shape, NEG_BIG, F32)
    acc_scr[...] = jnp.zeros_like(acc_scr)

    @pl.when(j == 0)
    def _():
        scores_into(2, 0)
        consume(2, 0, True)
        prefetch_next_tile()

    def pair(i, carry):
        scores_into(1, 2 * i + 2)
        consume(0, 2 * i + 1, False)
        scores_into(0, 2 * i + 3)
        consume(1, 2 * i + 2, False)
        return carry

    @pl.when(j > 0)
    def _():
        scores_into(0, 1)
        consume(2, 0, False)
        lax.fori_loop(0, (j - 1) // 2, pair, 0)

        @pl.when(j % 2 == 1)
        def _():
            prefetch_next_tile()
            consume(0, j, True)

        @pl.when(j % 2 == 0)
        def _():
            scores_into(1, j)
            consume(0, j - 1, False)
            prefetch_next_tile()
            consume(1, j, True)

    o_t = jnp.concatenate([acc_scr[h, :HEAD_DIM, :] / acc_scr[h, HEAD_DIM:HEAD_DIM + 1, :] for h in range(nh)], axis=0)
    o_ref[...] = jnp.transpose(o_t)


def _fox_prompt_call(qaug, kaug, vt16, *, ta, nh):
    b, _, t, wq = qaug.shape
    nt = t // ta
    return pl.pallas_call(
        functools.partial(_fox_prompt_kernel, ta=ta, nh=nh),
        grid=(b, B_HEADS // nh, nt),
        in_specs=[
            pl.BlockSpec((None, nh, ta, wq), lambda bb, hg, j: (bb, hg, j, 0)),
            pl.BlockSpec((None, nh, ta, wq), lambda bb, hg, j: (bb, hg, jnp.minimum(j + 1, nt - 1), 0)),
            pl.BlockSpec((None, nh // 2, t, wq), lambda bb, hg, j: (bb, hg, 0, 0)),
            pl.BlockSpec((None, nh, t // ta, V_ROWS, ta), lambda bb, hg, j: (bb, hg, 0, 0, 0)),
        ],
        out_specs=pl.BlockSpec((None, ta, nh * HEAD_DIM), lambda bb, hg, j: (bb, j, hg)),
        out_shape=jax.ShapeDtypeStruct((b, t, B_WIDTH), F32),
        scratch_shapes=[
            pltpu.VMEM((3, nh, ta, ta), F32),
            pltpu.VMEM((nh, 1, ta), F32),
            pltpu.VMEM((nh, V_ROWS, ta), F32),
        ],
        compiler_params=pltpu.CompilerParams(
            dimension_semantics=("arbitrary", "arbitrary", "arbitrary"), vmem_limit_bytes=VMEM_LIMIT),
        name="fox_prompt",
    )(qaug, qaug, kaug, vt16)


def _proj_sample_kernel(
        x_ref, sh_ref, sc_ref, g_ref, wa_ref, wb_ref, wf_ref, wc_ref,
        gav_ref, gq_ref, gk_ref, bf_ref, gma_ref, gmc_ref, cos_ref, sin_ref,
        wst_ref, bsp_ref, dintra_ref, dq_ref, dk_ref, dch_ref, seg_ref, s0_ref,
        k32_ref, v32_ref, logf_ref, q32_ref, av_ref, oa_ref, oc_ref, sret_ref, *, ns, ts, db):
    x = x_ref[...]
    hb = _modnorm(x, g_ref[...], sh_ref[...], sc_ref[...]).astype(BF16)
    seg = seg_ref[...]

    za = _dot(hb, wa_ref[...])
    a_u = _gelu_tanh(za[:, :A_WIDTH])
    a_vr = _gelu_tanh(za[:, A_WIDTH:])
    a_v = a_vr * lax.rsqrt(jnp.mean(a_vr * a_vr, axis=-1, keepdims=True) + EPS) * gav_ref[...]
    av_ref[...] = a_v

    zb = _dot(hb, wb_ref[...])
    q32_ref[...] = _head_norm(zb[:, :B_WIDTH], seg) * gq_ref[...]
    k32_ref[...] = _head_norm(zb[:, B_WIDTH:2 * B_WIDTH], seg) * gk_ref[...]
    v32_ref[...] = zb[:, 2 * B_WIDTH:]
    logf = _log_sigmoid(_dot(hb, wf_ref[...]) + bf_ref[...])
    logf_ref[...] = logf[:, :B_HEADS]

    zc = _dot(hb, wc_ref[...])
    cos = cos_ref[...]
    sin = sin_ref[...]
    q_c = _rope(zc[:, :C_WIDTH], cos, sin)
    k_c = _rope(zc[:, C_WIDTH:2 * C_WIDTH], cos, sin) * (HEAD_DIM ** -0.5)
    v_c = zc[:, 2 * C_WIDTH:3 * C_WIDTH]
    c_g = zc[:, 3 * C_WIDTH:]

    row = lax.broadcasted_iota(jnp.int32, (ns, ns), 0)
    col = lax.broadcasted_iota(jnp.int32, (ns, ns), 1)
    keep = (row // ts == col // ts) & (col <= row)
    w_cat = jnp.concatenate(
        [jnp.where(keep, wst_ref[h], 0.0) for h in range(A_HEADS)], axis=1).astype(BF16)
    av_blk = _tile_rows_masked(a_v, A_HEADS, ns, HEAD_DIM).astype(BF16)
    o_a = a_u * (_dot(w_cat, av_blk) + bsp_ref[...])

    qc = q_c.astype(BF16)
    k_blk = _tile_rows_masked(k_c, C_HEADS, ns, HEAD_DIM).astype(BF16)
    v_blk = _tile_rows_masked(v_c, C_HEADS, ns, HEAD_DIM).astype(BF16)
    att = _dot_nt(qc, k_blk) * dintra_ref[...]
    intra = _dot(att.astype(BF16), v_blk)
    s0_rows = s0_ref[...]
    row_head = (lax.broadcasted_iota(jnp.int32, (db * C_WIDTH, C_WIDTH), 0) // HEAD_DIM) % C_HEADS
    lane_head = lax.broadcasted_iota(jnp.int32, (db * C_WIDTH, C_WIDTH), 1) // HEAD_DIM
    own = row_head == lane_head
    s0 = jnp.where(own, jnp.concatenate([s0_rows] * C_HEADS, axis=1), 0.0)
    q_exp = _tile_cols_masked(q_c, db, ts, C_WIDTH).astype(BF16)
    inter = _dot(q_exp, s0.astype(BF16)) * dq_ref[...]
    o_c = intra + inter
    kd_t = jnp.transpose(k_c * dk_ref[...])
    kd_exp = _tile_rows_masked(kd_t, db, C_WIDTH, ts).astype(BF16)
    s_new = jnp.where(own, s0 * dch_ref[...] + _dot(kd_exp, v_c.astype(BF16)), 0.0)
    folded = s_new[:, :HEAD_DIM]
    for h in range(1, C_HEADS):
        folded = folded + s_new[:, h * HEAD_DIM:(h + 1) * HEAD_DIM]
    sret_ref[...] = folded

    oa_ref[...] = (_head_norm(o_a, seg) * gma_ref[...]).astype(BF16)
    oc_ref[...] = (_head_norm(o_c, seg) * gmc_ref[...] * _silu(c_g)).astype(BF16)


def _proj_sample_call(x2, mod_rows, g, w, cst, s0_rows, *, ts, db):
    ns, d = x2.shape
    in_arrays = [
        x2, mod_rows[:, 3 * d:4 * d], mod_rows[:, 4 * d:5 * d], g,
        w["wa"], w["wb"], w["wf"], w["wc"],
        w["gav"], w["gq"], w["gk"], w["bf"], w["gma"], w["gmc"], cst["cos_s"], cst["sin_s"],
        w["wst"], w["bsp_s"], cst["dintra_s"], cst["dq_s"], cst["dk_s"], cst["dch_s"], cst["seg"], s0_rows,
    ]
    out_shape = [
        jax.ShapeDtypeStruct((ns, B_WIDTH), F32), jax.ShapeDtypeStruct((ns, B_WIDTH), F32),
        jax.ShapeDtypeStruct((ns, B_HEADS), F32), jax.ShapeDtypeStruct((ns, B_WIDTH), F32),
        jax.ShapeDtypeStruct((ns, A_WIDTH), F32),
        jax.ShapeDtypeStruct((ns, A_WIDTH), BF16), jax.ShapeDtypeStruct((ns, C_WIDTH), BF16),
        jax.ShapeDtypeStruct(s0_rows.shape, F32),
    ]
    return pl.pallas_call(
        functools.partial(_proj_sample_kernel, ns=ns, ts=ts, db=db),
        out_shape=out_shape,
        compiler_params=pltpu.CompilerParams(vmem_limit_bytes=VMEM_LIMIT),
        name="mix_proj_sample",
    )(*in_arrays)


def _fox_sample_kernel(pt_ref, q_ref, kn_ref, vn_ref, lfn_ref, triu_ref, *rest, ts, gp, page):
    kp_refs = rest[:gp]
    vp_refs = rest[gp:2 * gp]
    lf_ref = rest[2 * gp]
    o_ref = rest[2 * gp + 1]
    qb_scr, qf_scr, m_scr, l_scr, acc_scr, cr_scr = rest[2 * gp + 2:]
    pg = pl.program_id(1)
    seq = pl.program_id(0)
    nrow = ts * B_HEADS
    hmask = (lax.broadcasted_iota(jnp.int32, (B_HEADS, B_WIDTH), 1) // HEAD_DIM
             == lax.broadcasted_iota(jnp.int32, (B_HEADS, B_WIDTH), 0))

    @pl.when(pg == 0)
    def _():
        q = q_ref[...]
        for t in range(ts):
            qt = jnp.where(hmask, jnp.broadcast_to(q[t:t + 1, :], (B_HEADS, B_WIDTH)), 0.0)
            qf_scr[t * B_HEADS:(t + 1) * B_HEADS, :] = qt
            qb_scr[t * B_HEADS:(t + 1) * B_HEADS, :] = qt.astype(BF16)
        m_scr[...] = jnp.full(m_scr.shape, NEG_BIG, F32)
        l_scr[...] = jnp.zeros_like(l_scr)
        acc_scr[...] = jnp.zeros_like(acc_scr)
        cr_scr[...] = jnp.zeros_like(cr_scr)

    qb = qb_scr[...]
    m = m_scr[...]
    l = l_scr[...]
    acc = acc_scr[...]
    carry = cr_scr[:, 0:1]
    lf_all = jnp.concatenate([lf_ref[pt_ref[seq, pg * gp + g]] for g in range(gp)], axis=0)
    local = _dot_split_lhs(lf_all, triu_ref[...], 3)
    bias = []
    for g in range(gp):
        cum_t = local[g * B_HEADS:(g + 1) * B_HEADS] + carry
        carry = cum_t[:, page - 1:page]
        bias.append(jnp.concatenate([cum_t] * ts, axis=0))
    per = max(1, gp // SAMPLE_PAGE_GROUPS)
    groups = [range(i, i + per) for i in range(0, gp, per)]
    scores = []
    for grp in groups:
        k_grp = jnp.concatenate([kp_refs[g][...].astype(BF16) for g in grp], axis=1)
        scores.append(_dot(qb, k_grp) - jnp.concatenate([bias[g] for g in grp], axis=1))
    parts = []
    for grp, s in zip(groups, scores):
        m_g = jnp.max(s, axis=-1, keepdims=True)
        p = jnp.exp(s - m_g)
        v_grp = jnp.concatenate([vp_refs[g][...].astype(BF16) for g in grp], axis=1)
        parts.append((m_g, jnp.sum(p, axis=-1, keepdims=True), _dot_nt(p.astype(BF16), v_grp)))
    m_new = m
    for m_g, _, _ in parts:
        m_new = jnp.maximum(m_new, m_g)
    alpha = jnp.exp(m - m_new)
    l = alpha * l
    acc = alpha * acc
    for m_g, l_g, a_g in parts:
        w_g = jnp.exp(m_g - m_new)
        l = l + w_g * l_g
        acc = acc + w_g * a_g
    m = m_new
    m_scr[...] = m
    l_scr[...] = l
    acc_scr[...] = acc
    cr_scr[...] = jnp.broadcast_to(carry, cr_scr.shape)

    @pl.when(pg == pl.num_programs(1) - 1)
    def _():
        qf = qf_scr[...]
        kn = kn_ref[...]
        vn = vn_ref[...]
        lfn = lfn_ref[...]
        tok = lax.broadcasted_iota(jnp.int32, (nrow, 1), 0) // B_HEADS
        run = carry
        s_new = []
        for jn in range(ts):
            run = run + lfn[:, jn:jn + 1]
            sj = jnp.sum(qf * kn[jn:jn + 1, :], axis=-1, keepdims=True) - jnp.concatenate([run] * ts, axis=0)
            s_new.append(jnp.where(tok >= jn, sj, NEG_BIG))
        m2 = m
        for sj in s_new:
            m2 = jnp.maximum(m2, sj)
        alpha = jnp.exp(m - m2)
        l2 = alpha * l
        acc2 = alpha * acc
        for jn in range(ts):
            pj = jnp.exp(s_new[jn] - m2)
            l2 = l2 + pj
            acc2 = acc2 + pj * vn[jn:jn + 1, :]
        o_full = acc2 / l2
        for t in range(ts):
            blk = jnp.where(hmask, o_full[t * B_HEADS:(t + 1) * B_HEADS, :], 0.0)
            o_ref[t:t + 1, :] = jnp.sum(blk, axis=0, keepdims=True)


def _fox_sample_call(page_table, q3, kn3, vn3, lfn_t, triu, kc, vc, lfc_t, layer, *, gp):
    db, ts, _ = q3.shape
    n_pages = page_table.shape[1]
    page = kc.shape[3]
    nrow = ts * B_HEADS
    seq3 = lambda shape: pl.BlockSpec((None,) + shape, lambda b, pg, pt: (b, 0, 0))

    def page_spec(shape, g):
        return pl.BlockSpec((None, None) + shape, lambda b, pg, pt, g=g: (layer, pt[b, pg * gp + g], 0, 0))

    in_specs = [seq3((ts, B_WIDTH)), seq3((ts, B_WIDTH)), seq3((ts, B_WIDTH)), seq3((B_HEADS, ts)),
                pl.BlockSpec((page, page), lambda b, pg, pt: (0, 0))]
    in_specs += [page_spec((B_WIDTH, page), g) for g in range(gp)]
    in_specs += [page_spec((B_WIDTH, page), g) for g in range(gp)]
    in_specs += [pl.BlockSpec((None,) + lfc_t.shape[1:], lambda b, pg, pt: (layer, 0, 0, 0),
                              pipeline_mode=pl.Buffered(1))]
    grid_spec = pltpu.PrefetchScalarGridSpec(
        num_scalar_prefetch=1,
        grid=(db, n_pages // gp),
        in_specs=in_specs,
        out_specs=pl.BlockSpec((None, ts, B_WIDTH), lambda b, pg, pt: (b, 0, 0)),
        scratch_shapes=[
            pltpu.VMEM((nrow, B_WIDTH), BF16), pltpu.VMEM((nrow, B_WIDTH), F32),
            pltpu.VMEM((nrow, 1), F32), pltpu.VMEM((nrow, 1), F32),
            pltpu.VMEM((nrow, B_WIDTH), F32), pltpu.VMEM((B_HEADS, LANES), F32),
        ],
    )
    return pl.pallas_call(
        functools.partial(_fox_sample_kernel, ts=ts, gp=gp, page=page),
        grid_spec=grid_spec,
        out_shape=jax.ShapeDtypeStruct((db, ts, B_WIDTH), F32),
        compiler_params=pltpu.CompilerParams(
            dimension_semantics=("arbitrary", "arbitrary"), vmem_limit_bytes=VMEM_LIMIT),
        name="fox_sample",
    )(page_table, q3, kn3, vn3, lfn_t, triu, *([kc] * gp), *([vc] * gp), lfc_t)


def _rope_tables(pos):
    half = HEAD_DIM // 2
    inv = ROPE_BASE ** (-jnp.arange(half, dtype=F32) / half)
    ang = pos[:, None] * inv[None, :]
    cos = jnp.cos(ang)
    sin = jnp.sin(ang)
    cos_h = jnp.concatenate([cos, cos], axis=-1)
    sin_h = jnp.concatenate([-sin, sin], axis=-1)
    return jnp.tile(cos_h, (1, C_HEADS)), jnp.tile(sin_h, (1, C_HEADS))


def _decay_tables(length, log_gamma):
    idx = jnp.arange(length, dtype=F32)
    rel = idx[:, None] - idx[None, :]
    d_intra = jnp.where(rel[None] >= 0, jnp.exp(jnp.maximum(rel, 0.0)[None] * log_gamma[:, None, None]), 0.0)
    d_q = jnp.exp((idx + 1.0)[:, None] * log_gamma[None, :])
    d_k = jnp.exp((length - 1.0 - idx)[:, None] * log_gamma[None, :])
    d_chunk = jnp.exp(length * log_gamma)
    return d_intra, d_q, d_k, d_chunk


def _constants(t, ts, db, past, tm):
    log_gamma = jnp.log1p(-jnp.exp2(-5.0 - jnp.arange(C_HEADS, dtype=F32)))
    cst = {}
    cst["cos_p"], cst["sin_p"] = _rope_tables(jnp.arange(t, dtype=F32))
    cos_s, sin_s = _rope_tables(past + jnp.arange(ts, dtype=F32))
    cst["cos_s"], cst["sin_s"] = jnp.tile(cos_s, (db, 1)), jnp.tile(sin_s, (db, 1))
    rep = lambda a: jnp.repeat(a, HEAD_DIM, axis=-1)

    d_intra, d_q, d_k, d_chunk = _decay_tables(CHUNK, log_gamma)
    cst["dintra_p"] = jnp.concatenate([d_intra[h] for h in range(C_HEADS)], axis=1)
    cst["dq_p"], cst["dk_p"], cst["dch_p"] = rep(d_q), rep(d_k), rep(d_chunk[None, :])

    d_intra, d_q, d_k, d_chunk = _decay_tables(ts, log_gamma)
    same_seq = jnp.kron(jnp.eye(db, dtype=F32), jnp.ones((ts, ts), F32))
    cst["dintra_s"] = jnp.concatenate(
        [jnp.tile(d_intra[h], (db, db)) * same_seq for h in range(C_HEADS)], axis=1)
    cst["dq_s"] = jnp.tile(rep(d_q), (db, 1))
    cst["dk_s"] = jnp.tile(rep(d_k), (db, 1))
    cst["dch_s"] = rep(d_chunk[None, :])

    seg = jnp.kron(jnp.eye(MXU_DIM // HEAD_DIM, dtype=F32), jnp.ones((HEAD_DIM, HEAD_DIM), F32))
    cst["seg"] = seg.astype(BF16)
    cst["tril"] = jnp.tril(jnp.ones((tm, tm), F32)).astype(BF16)

    nb = NPIECE * B_HEADS
    place = np.zeros((NPIECE * LANES, 2 * LANES), np.float32)
    pconst = np.zeros((1, 2 * LANES), np.float32)
    pconst[0, :nb] = 1.0
    pconst[0, LANES + nb:LANES + 2 * nb] = -1.0
    for h in range(B_HEADS):
        for a in range(NPIECE):
            place[a * LANES + h, nb + NPIECE * h + a] = 1.0
            place[a * LANES + h, LANES + NPIECE * h + a] = 1.0
    cst["place"], cst["pconst"] = jnp.asarray(place, BF16), jnp.asarray(pconst)
    return cst


def _layer_weights(l, w_in, g_a_v, w_spatial, b_spatial, g_qnorm, g_knorm, b_forget, g_mix, ts, db):
    d = w_in.shape[1]
    wl = w_in[l]
    o = 0
    wa = wl[:, o:o + 2 * A_WIDTH]; o += 2 * A_WIDTH
    wb = wl[:, o:o + 3 * B_WIDTH]; o += 3 * B_WIDTH
    wf = wl[:, o:o + B_HEADS]; o += B_HEADS
    wc = wl[:, o:o + 4 * C_WIDTH]
    w = {
        "wa": wa.astype(BF16), "wb": wb.astype(BF16), "wc": wc.astype(BF16),
        "wf": jnp.pad(wf, ((0, 0), (0, LANES - B_HEADS))).astype(BF16),
        "gav": g_a_v[l][None, :],
        "gq": jnp.tile(g_qnorm[l], B_HEADS)[None, :] * (HEAD_DIM ** -0.5),
        "gk": jnp.tile(g_knorm[l], B_HEADS)[None, :],
        "bf": jnp.pad(b_forget[l], (0, LANES - B_HEADS))[None, :],
        "gma": g_mix[l][None, :A_WIDTH],
        "gmb": g_mix[l][None, A_WIDTH:A_WIDTH + B_WIDTH],
        "gmc": g_mix[l][None, A_WIDTH + B_WIDTH:],
        "ws": w_spatial[l],
        "bsp_p": jnp.repeat(b_spatial[l].T, HEAD_DIM, axis=-1),
        "wst": jnp.tile(w_spatial[l][:, :ts, :ts], (1, db, db)),
        "bsp_s": jnp.tile(jnp.repeat(b_spatial[l][:, :ts].T, HEAD_DIM, axis=-1), (db, 1)),
    }
    del d
    return w


def kernel(x_prompt, x_sample, cache_k, cache_v, cache_logf, state_ret, page_table, c_prompt, c_sample,
           g_norm, w_ada, b_ada, w_ffn_gate, w_ffn_up, w_ffn_down, w_in, g_a_v, w_spatial, b_spatial,
           g_qnorm, g_knorm, b_forget, g_mix, w_out):
    b, t, d = x_prompt.shape
    db, ts, _ = x_sample.shape
    depth = w_in.shape[0]
    ns = db * ts
    n_pool, page = cache_k.shape[1], cache_k.shape[2]
    n_pages = page_table.shape[1]
    past = n_pages * page

    tm = min(512, t)
    ta = min(512, t)
    nf = 2
    gp = min(16, n_pages)

    cst = _constants(t, ts, db, past, tm)
    triu_page = jnp.triu(jnp.ones((page, page), F32)).astype(BF16)

    mod = _ada_call(jnp.concatenate([c_prompt, c_sample], axis=0), w_ada, b_ada)

    kc = jnp.transpose(cache_k, (0, 1, 3, 4, 2)).reshape(depth, n_pool, B_WIDTH, page)
    vc = jnp.transpose(cache_v, (0, 1, 3, 4, 2)).reshape(depth, n_pool, B_WIDTH, page)
    lfc_t = jnp.swapaxes(cache_logf, 2, 3)

    ffn_w = (w_ffn_gate.astype(BF16), w_ffn_up.astype(BF16), w_ffn_down.astype(BF16))
    w_out_bf = w_out.astype(BF16)

    x_p = x_prompt.reshape(b * t, d)
    x_s = x_sample.reshape(ns, d)
    outs = {k: [] for k in ("lp", "rp", "ks", "vs", "ls", "rs", "av")}
    kv_prev = None
    for l in range(depth):
        mod_p = mod[l, :b].reshape(b, 1, -1)
        mod_s = jnp.repeat(mod[l, b:], ts, axis=0)
        w = _layer_weights(l, w_in, g_a_v, w_spatial, b_spatial, g_qnorm, g_knorm, b_forget, g_mix, ts, db)
        prompt_cfg = dict(per_row=False, tm=tm, seq_len=t, nf=nf)
        sample_cfg = dict(per_row=True, tm=ns, seq_len=ts, nf=nf)

        x_p = _ffn_call(x_p, mod_p, 0, g_norm[l, 0][None, :], *ffn_w, l, 0, **prompt_cfg)
        x_s = _ffn_call(x_s, mod_s, 0, g_norm[l, 0][None, :], *ffn_w, l, 0, **sample_cfg)

        (kt32, vt32, logf, qaug, kaug, vt16, oa, oc, sret) = _proj_prompt_call(
            x_p.reshape(b, t, d), mod_p, g_norm[l, 1][None, :], w, cst, kv_prev, tm=tm, ta=ta)
        kv_prev = (kt32, vt32)
        ob = _fox_prompt_call(qaug, kaug, vt16, ta=ta, nh=FOX_HEADS_PER_STEP)
        merge_p = (oa.reshape(b * t, -1), ob.reshape(b * t, -1), oc.reshape(b * t, -1),
                   w["gmb"], cst["seg"], w_out_bf)
        outs["lp"].append(logf)
        outs["rp"].append(sret.reshape(b, C_HEADS, HEAD_DIM, HEAD_DIM))

        s0_rows = state_ret[l].reshape(db * C_WIDTH, HEAD_DIM)
        (k32s, v32s, logfs, q32s, avs, oas, ocs, srets) = _proj_sample_call(
            x_s, mod_s, g_norm[l, 1][None, :], w, cst, s0_rows, ts=ts, db=db)
        obs = _fox_sample_call(
            page_table, q32s.reshape(db, ts, -1), k32s.reshape(db, ts, -1), v32s.reshape(db, ts, -1),
            jnp.swapaxes(logfs.reshape(db, ts, B_HEADS), 1, 2), triu_page, kc, vc, lfc_t, l, gp=gp)
        merge_s = (oas, obs.reshape(ns, -1), ocs, w["gmb"], cst["seg"], w_out_bf)
        outs["ks"].append(k32s.reshape(db, ts, B_HEADS, HEAD_DIM))
        outs["vs"].append(v32s.reshape(db, ts, B_HEADS, HEAD_DIM))
        outs["ls"].append(logfs.reshape(db, ts, B_HEADS))
        outs["rs"].append(srets.reshape(db, C_HEADS, HEAD_DIM, HEAD_DIM))
        outs["av"].append(avs.reshape(db, ts, A_WIDTH))

        x_p = _ffn_call(x_p, mod_p, 2, g_norm[l, 2][None, :], *ffn_w, l, 1, merge=merge_p, **prompt_cfg)
        x_s = _ffn_call(x_s, mod_s, 2, g_norm[l, 2][None, :], *ffn_w, l, 1, merge=merge_s, **sample_cfg)

    st = lambda key: jnp.stack(outs[key])
    kv_out = lambda a: jnp.transpose(a.reshape(depth, b, B_HEADS, HEAD_DIM, t), (0, 1, 4, 2, 3))
    return (x_p.reshape(b, t, d), x_s.reshape(db, ts, d), kv_out(kv_prev[0]), kv_out(kv_prev[1]), st("lp"), st("rp"),
            st("ks"), st("vs"), st("ls"), st("rs"), st("av"))
```
